```python
import jax, jax.numpy as jnp
from jax import lax
import numpy as np

D_MODEL = 1024
BATCH = 8
SEQ = 4096
DEPTH = 4
DEC_BATCH = 4
DEC_SEQ = 8192
PAST_LEN = 128

HEAD_DIM = 64
N_HEADS = D_MODEL // 2 // HEAD_DIM
N_KV = 2
GROUP = N_HEADS // N_KV
ATT_W = N_HEADS * HEAD_DIM
KV_W = N_KV * HEAD_DIM
N_FG = 4
FG_W = 128
FOUR_W = N_FG * FG_W
N_BRANCH = 2
IN_COLS = ATT_W + 2 * KV_W + FOUR_W + N_BRANCH * D_MODEL
D_FF = 2816
WINDOW = 128
BLOCK = 128
EPS = 1e-6
NEG = -1e30

kernel_name = "hybrid_swa_fnet_macaron_encoder"


def rmsnorm(x, g):
    xf = x.astype(jnp.float32)
    y = xf * lax.rsqrt(jnp.mean(xf * xf, axis=-1, keepdims=True) + EPS)
    return (y * g.astype(jnp.float32)).astype(x.dtype)


def swiglu(h, w_in, w_out):
    gu = h @ w_in
    g, u = jnp.split(gu, 2, axis=-1)
    return (jax.nn.silu(g) * u) @ w_out


def alibi_slopes():
    s = np.power(2.0, -8.0 * (np.arange(N_HEADS) + 1) / N_HEADS).astype(np.float32)
    return jnp.asarray(s).reshape(N_KV, GROUP)


def windowed_gqa(q, k, v, q_gain, k_gain, sink):
    B, S = q.shape[0], q.shape[1]
    nb = S // BLOCK
    q = rmsnorm(q, q_gain) * jnp.asarray(HEAD_DIM ** -0.5, q.dtype)
    k = rmsnorm(k, k_gain)
    qb = q.reshape(B, nb, BLOCK, N_KV, GROUP, HEAD_DIM)
    pad = ((0, 0), (BLOCK, BLOCK), (0, 0), (0, 0))
    kp = jnp.pad(k, pad).reshape(B, nb + 2, BLOCK, N_KV, HEAD_DIM)
    vp = jnp.pad(v, pad).reshape(B, nb + 2, BLOCK, N_KV, HEAD_DIM)
    kw = jnp.concatenate([kp[:, :-2], kp[:, 1:-1], kp[:, 2:]], axis=2)
    vw = jnp.concatenate([vp[:, :-2], vp[:, 1:-1], vp[:, 2:]], axis=2)
    s = jnp.einsum('bnqkgd,bnskd->bnkgqs', qb, kw,
                   preferred_element_type=jnp.float32)
    qi = jnp.arange(BLOCK)
    si = jnp.arange(3 * BLOCK)
    dist = jnp.abs(qi[:, None] + BLOCK - si[None, :])
    key_pos = jnp.arange(nb)[:, None] * BLOCK - BLOCK + si[None, :]
    valid = (dist <= WINDOW)[None, :, :] & ((key_pos >= 0) & (key_pos < S))[:, None, :]
    slopes = alibi_slopes()[:, :, None, None]
    s = s - slopes * dist.astype(jnp.float32)
    s = jnp.where(valid[None, :, None, None], s, NEG)
    sink_l = sink.astype(jnp.float32).reshape(N_KV, GROUP)[:, :, None, None]
    m = jnp.maximum(jnp.max(s, axis=-1, keepdims=True), sink_l)
    p = jnp.exp(s - m)
    denom = jnp.sum(p, axis=-1, keepdims=True) + jnp.exp(sink_l - m)
    p = (p / denom).astype(v.dtype)
    o = jnp.einsum('bnkgqs,bnskd->bnqkgd', p, vw)
    return o.reshape(B, S, ATT_W)


def fourier_mix(f):
    B, S = f.shape[0], f.shape[1]
    fg = f.reshape(B, S, N_FG, FG_W).astype(jnp.float32)
    out = jnp.fft.fft2(fg, axes=(1, 3), norm='ortho').real
    return out.reshape(B, S, FOUR_W).astype(f.dtype)


def encoder_layer(x, ln_ffn1, w_ffn1_in, w_ffn1_out, ln_mix, w_in, q_gain, k_gain,
                  sink, w_attn_br, w_four_br, w_out, ln_ffn2, w_ffn2_in, w_ffn2_out):
    B, S, _ = x.shape
    half = jnp.asarray(0.5, x.dtype)
    x = x + half * swiglu(rmsnorm(x, ln_ffn1), w_ffn1_in, w_ffn1_out)
    h = rmsnorm(x, ln_mix)
    z = h @ w_in
    o1 = ATT_W
    o2 = o1 + KV_W
    o3 = o2 + KV_W
    o4 = o3 + FOUR_W
    o5 = o4 + D_MODEL
    q = z[..., :o1].reshape(B, S, N_HEADS, HEAD_DIM)
    k = z[..., o1:o2].reshape(B, S, N_KV, HEAD_DIM)
    v = z[..., o2:o3].reshape(B, S, N_KV, HEAD_DIM)
    f = z[..., o3:o4]
    g_a = jax.nn.sigmoid(z[..., o4:o5])
    g_f = jax.nn.sigmoid(z[..., o5:])
    a = windowed_gqa(q, k, v, q_gain, k_gain, sink) @ w_attn_br
    fo = fourier_mix(f) @ w_four_br
    x = x + (g_a * a + g_f * fo) @ w_out
    x = x + half * swiglu(rmsnorm(x, ln_ffn2), w_ffn2_in, w_ffn2_out)
    return x


def trunk(x, ln_ffn1, w_ffn1_in, w_ffn1_out, ln_mix, w_in, q_gain, k_gain, sink,
          w_attn_br, w_four_br, w_out, ln_ffn2, w_ffn2_in, w_ffn2_out):
    for l in range(DEPTH):
        x = encoder_layer(x, ln_ffn1[l], w_ffn1_in[l], w_ffn1_out[l], ln_mix[l], w_in[l],
                          q_gain[l], k_gain[l], sink[l], w_attn_br[l], w_four_br[l],
                          w_out[l], ln_ffn2[l], w_ffn2_in[l], w_ffn2_out[l])
    return x


def setup_inputs(seed: int = 0) -> dict:
    key = jax.random.key(seed)
    ks = jax.random.split(key, 20)
    f32 = jnp.float32

    def w(k, shape, fan_in):
        return jax.random.normal(k, shape, f32) * (fan_in ** -0.5)

    def gain(k, shape):
        return 1.0 + 0.02 * jax.random.normal(k, shape, f32)

    return {
        'x_prompt': jax.random.normal(ks[0], (BATCH, SEQ, D_MODEL), f32),
        'x_sample': jax.random.normal(ks[1], (DEC_BATCH, DEC_SEQ, D_MODEL), f32),
        'ln_ffn1': gain(ks[2], (DEPTH, D_MODEL)),
        'w_ffn1_in': w(ks[3], (DEPTH, D_MODEL, 2 * D_FF), D_MODEL),
        'w_ffn1_out': w(ks[4], (DEPTH, D_FF, D_MODEL), D_FF),
        'ln_mix': gain(ks[5], (DEPTH, D_MODEL)),
        'w_in': w(ks[6], (DEPTH, D_MODEL, IN_COLS), D_MODEL),
        'q_gain': gain(ks[7], (DEPTH, HEAD_DIM)),
        'k_gain': gain(ks[8], (DEPTH, HEAD_DIM)),
        'sink': 0.5 * jax.random.normal(ks[9], (DEPTH, N_HEADS), f32),
        'w_attn_br': w(ks[10], (DEPTH, ATT_W, D_MODEL), ATT_W),
        'w_four_br': w(ks[11], (DEPTH, FOUR_W, D_MODEL), FOUR_W),
        'w_out': w(ks[12], (DEPTH, D_MODEL, D_MODEL), D_MODEL),
        'ln_ffn2': gain(ks[13], (DEPTH, D_MODEL)),
        'w_ffn2_in': w(ks[14], (DEPTH, D_MODEL, 2 * D_FF), D_MODEL),
        'w_ffn2_out': w(ks[15], (DEPTH, D_FF, D_MODEL), D_FF),
    }


def reference(x_prompt, x_sample, ln_ffn1, w_ffn1_in, w_ffn1_out, ln_mix, w_in,
              q_gain, k_gain, sink, w_attn_br, w_four_br, w_out, ln_ffn2,
              w_ffn2_in, w_ffn2_out):
    y_prompt = trunk(x_prompt, ln_ffn1, w_ffn1_in, w_ffn1_out, ln_mix, w_in, q_gain,
                     k_gain, sink, w_attn_br, w_four_br, w_out, ln_ffn2, w_ffn2_in,
                     w_ffn2_out)
    y_sample = trunk(x_sample, ln_ffn1, w_ffn1_in, w_ffn1_out, ln_mix, w_in, q_gain,
                     k_gain, sink, w_attn_br, w_four_br, w_out, ln_ffn2, w_ffn2_in,
                     w_ffn2_out)
    return (y_prompt, y_sample)
```

```python
import functools

import numpy as np
import jax
import jax.numpy as jnp
from jax import lax
from jax.experimental import pallas as pl
from jax.experimental.pallas import tpu as pltpu

D_MODEL = 1024
DEPTH = 4
HEAD_DIM = 64
N_HEADS = 8
N_KV = 2
GROUP = N_HEADS // N_KV
ATT_W = N_HEADS * HEAD_DIM
KV_W = N_KV * HEAD_DIM
N_FG = 4
FG_W = 128
FOUR_W = N_FG * FG_W
QKVF_W = ATT_W + 2 * KV_W + FOUR_W
D_FF = 2816
WINDOW = 128
BLOCK = 128
EPS = 1e-6
NEG = -1e30

LANES = 128
FFT_N1 = 64
VMEM_LIMIT = 56 * 1024 * 1024

BF16 = jnp.bfloat16
F32 = jnp.float32


def _dot(a, b):
    return jnp.dot(a, b, preferred_element_type=F32)


def _rms_rows(x, gain):
    ms = jnp.mean(x * x, axis=-1, keepdims=True)
    return x * lax.rsqrt(ms + EPS) * gain


def _resident(shape):
    nd = len(shape)
    return pl.BlockSpec(shape, lambda *_: (0,) * nd, pipeline_mode=pl.Buffered(1))


def _ffn_kernel(x_ref, gain_ref, wg_ref, wu_ref, wo_ref, o_ref, *, chunk):
    x = x_ref[...]
    h = _rms_rows(x, gain_ref[...]).astype(BF16)
    acc = jnp.zeros(x.shape, F32)
    for c in range(D_FF // chunk):
        sl = slice(c * chunk, (c + 1) * chunk)
        g = _dot(h, wg_ref[:, sl])
        u = _dot(h, wu_ref[:, sl])
        a = (jax.nn.silu(g) * u).astype(BF16)
        acc = acc + _dot(a, wo_ref[sl, :])
    o_ref[...] = x + 0.5 * acc


def _ffn(x, gain, w_in, w_out, *, tm=512, chunk=256):
    t = x.shape[0]
    return pl.pallas_call(
        functools.partial(_ffn_kernel, chunk=chunk),
        out_shape=jax.ShapeDtypeStruct(x.shape, F32),
        grid=(t // tm,),
        in_specs=[
            pl.BlockSpec((tm, D_MODEL), lambda i: (i, 0)),
            _resident((1, D_MODEL)),
            pl.BlockSpec((D_MODEL, D_FF), lambda i: (0, 0), pipeline_mode=pl.Buffered(1)),
            pl.BlockSpec((D_MODEL, D_FF), lambda i: (0, 1), pipeline_mode=pl.Buffered(1)),
            _resident((D_FF, D_MODEL)),
        ],
        out_specs=pl.BlockSpec((tm, D_MODEL), lambda i: (i, 0)),
        compiler_params=pltpu.CompilerParams(
            dimension_semantics=("arbitrary",), vmem_limit_bytes=VMEM_LIMIT),
        name="ffn",
    )(x, gain, w_in, w_in, w_out)


def _head_rms(t, ones_ref, gain):
    sq = t * t
    hi = sq.astype(BF16)
    lo = (sq - hi.astype(F32)).astype(BF16)
    ms = _dot(hi, ones_ref[...]) + _dot(lo, ones_ref[...])
    return t * lax.rsqrt(ms + EPS) * gain


def _inproj_kernel(x_ref, gain_ref, w_ref, onesq_ref, onesk_ref, qg_ref, kg_ref,
                   q_ref, kx_ref, vx_ref, f_ref):
    x = x_ref[...]
    h = _rms_rows(x, gain_ref[...]).astype(BF16)
    z = _dot(h, w_ref[...])
    q = z[:, :ATT_W]
    k = z[:, ATT_W:ATT_W + KV_W]
    v = z[:, ATT_W + KV_W:ATT_W + 2 * KV_W]
    f_ref[...] = z[:, ATT_W + 2 * KV_W:]
    q_ref[...] = (_head_rms(q, onesq_ref, qg_ref[...]) * (HEAD_DIM ** -0.5)).astype(BF16)

    kn = _head_rms(k, onesk_ref, kg_ref[...])
    lane = lax.broadcasted_iota(jnp.int32, kn.shape, 1)
    low = lane < HEAD_DIM
    ksw = pltpu.roll(kn, HEAD_DIM, axis=1)
    zero = jnp.zeros_like(kn)
    kx = jnp.concatenate([
        jnp.where(low, kn, zero),
        jnp.where(low, zero, ksw),
        jnp.where(low, ksw, zero),
        jnp.where(low, zero, kn),
    ], axis=1)
    kx_ref[...] = kx.astype(BF16)
    vx_ref[...] = jnp.concatenate([v, pltpu.roll(v, HEAD_DIM, axis=1)], axis=1).astype(BF16)


def _inproj(x, gain, w_qkvf, ones_q, ones_k, q_gain, k_gain, *, tm=512):
    t = x.shape[0]
    row = lambda w: pl.BlockSpec((tm, w), lambda i: (i, 0))
    return pl.pallas_call(
        _inproj_kernel,
        out_shape=(
            jax.ShapeDtypeStruct((t, ATT_W), BF16),
            jax.ShapeDtypeStruct((t, 4 * KV_W), BF16),
            jax.ShapeDtypeStruct((t, 2 * KV_W), BF16),
            jax.ShapeDtypeStruct((t, FOUR_W), F32),
        ),
        grid=(t // tm,),
        in_specs=[
            row(D_MODEL),
            _resident((1, D_MODEL)),
            _resident((D_MODEL, QKVF_W)),
            _resident((ATT_W, ATT_W)),
            _resident((KV_W, KV_W)),
            _resident((1, ATT_W)),
            _resident((1, KV_W)),
        ],
        out_specs=(row(ATT_W), row(4 * KV_W), row(2 * KV_W), row(FOUR_W)),
        compiler_params=pltpu.CompilerParams(
            dimension_semantics=("arbitrary",), vmem_limit_bytes=VMEM_LIMIT),
        name="inproj",
    )(x, gain, w_qkvf, ones_q, ones_k, q_gain, k_gain)


def _attn_kernel(sink_ref, q_ref, kx_ref, kxp_ref, kxn_ref, vx_ref, vxp_ref, vxn_ref,
                 bias_ref, o_ref, kbuf, vbuf, *, tq, seq):
    i = pl.program_id(1)
    kbuf[0:BLOCK] = kxp_ref[...]
    kbuf[BLOCK:BLOCK + tq] = kx_ref[...]
    kbuf[BLOCK + tq:] = kxn_ref[...]
    vbuf[0:BLOCK] = vxp_ref[...]
    vbuf[BLOCK:BLOCK + tq] = vx_ref[...]
    vbuf[BLOCK + tq:] = vxn_ref[...]

    win = 3 * BLOCK
    lane = lax.broadcasted_iota(jnp.int32, (BLOCK, LANES), 1)
    low = lane < HEAD_DIM
    for r in range(tq // BLOCK):
        rows = slice(r * BLOCK, (r + 1) * BLOCK)
        wrows = slice(r * BLOCK, r * BLOCK + win)
        pos = i * tq + (r - 1) * BLOCK + lax.broadcasted_iota(jnp.int32, (1, win), 1)
        valid = (pos >= 0) & (pos < seq)
        for j in range(N_HEADS // 2):
            qt = q_ref[rows, j * LANES:(j + 1) * LANES]
            grp = j // (GROUP // 2)
            halves = []
            for par in range(2):
                h = 2 * j + par
                kvar = 2 * grp + par
                kw = kbuf[wrows, kvar * LANES:(kvar + 1) * LANES]
                s = lax.dot_general(qt, kw, (((1,), (1,)), ((), ())),
                                    preferred_element_type=F32)
                s = jnp.where(valid, s + bias_ref[h], NEG)
                sink = sink_ref[h]
                m = jnp.maximum(jnp.max(s, axis=-1, keepdims=True), sink)
                p = jnp.exp(s - m)
                den = jnp.sum(p, axis=-1, keepdims=True) + jnp.exp(sink - m)
                vvar = grp ^ par
                vw = vbuf[wrows, vvar * LANES:(vvar + 1) * LANES]
                halves.append(_dot(p.astype(BF16), vw) / den)
            o_ref[rows, j * LANES:(j + 1) * LANES] = jnp.where(
                low, halves[0], halves[1]).astype(BF16)


def _attn(q, kx, vx, bias, sink, *, tq=512):
    b, s, _ = q.shape
    r = tq // BLOCK
    nblk = s // BLOCK
    main = lambda w: pl.BlockSpec((None, tq, w), lambda bi, i: (bi, i, 0))
    prev = lambda w: pl.BlockSpec(
        (None, BLOCK, w), lambda bi, i: (bi, jnp.maximum(i * r - 1, 0), 0))
    nxt = lambda w: pl.BlockSpec(
        (None, BLOCK, w), lambda bi, i: (bi, jnp.minimum((i + 1) * r, nblk - 1), 0))
    return pl.pallas_call(
        functools.partial(_attn_kernel, tq=tq, seq=s),
        out_shape=jax.ShapeDtypeStruct((b, s, ATT_W), BF16),
        grid=(b, s // tq),
        in_specs=[
            pl.BlockSpec(memory_space=pltpu.SMEM),
            main(ATT_W),
            main(4 * KV_W), prev(4 * KV_W), nxt(4 * KV_W),
            main(2 * KV_W), prev(2 * KV_W), nxt(2 * KV_W),
            pl.BlockSpec((N_HEADS, BLOCK, 3 * BLOCK), lambda bi, i: (0, 0, 0),
                         pipeline_mode=pl.Buffered(1)),
        ],
        out_specs=main(ATT_W),
        scratch_shapes=[
            pltpu.VMEM((tq + 2 * BLOCK, 4 * KV_W), BF16),
            pltpu.VMEM((tq + 2 * BLOCK, 2 * KV_W), BF16),
        ],
        compiler_params=pltpu.CompilerParams(
            dimension_semantics=("arbitrary", "arbitrary"), vmem_limit_bytes=VMEM_LIMIT),
        name="attn",
    )(sink, q, kx, kx, kx, vx, vx, vx, bias)


def _fourier_kernel(f_ref, l1_ref, m2_ref, cs_ref, o_ref, yr_ref, yi_ref, *, n1, n2):
    def stage1(s2, carry):
        a = f_ref[pl.ds(s2, n1, stride=n2), :].astype(BF16)
        y = _dot(l1_ref[s2], a)
        off = pl.multiple_of(s2 * n1, n1)
        yr_ref[pl.ds(off, n1), :] = y[:n1]
        yi_ref[pl.ds(off, n1), :] = y[n1:]
        return carry

    lax.fori_loop(0, n2, stage1, 0)

    def stage2(k1, carry):
        yr = yr_ref[pl.ds(k1, n2, stride=n1), :]
        yi = yi_ref[pl.ds(k1, n2, stride=n1), :]
        y = jnp.concatenate([yr, yi], axis=0).astype(BF16)
        x = _dot(m2_ref[...], y)
        xc = jnp.concatenate([x[:n2], x[n2:]], axis=1).astype(BF16)
        o_ref[pl.ds(k1, n2, stride=n1), :] = _dot(xc, cs_ref[...])
        return carry

    lax.fori_loop(0, n1, stage2, 0)


def _fourier(f, l1, m2, cs):
    b, s, _ = f.shape
    n1, n2 = FFT_N1, s // FFT_N1
    blk = pl.BlockSpec((None, s, FG_W), lambda bi, g: (bi, 0, g))
    return pl.pallas_call(
        functools.partial(_fourier_kernel, n1=n1, n2=n2),
        out_shape=jax.ShapeDtypeStruct(f.shape, F32),
        grid=(b, N_FG),
        in_specs=[blk, _resident(l1.shape), _resident(m2.shape), _resident(cs.shape)],
        out_specs=blk,
        scratch_shapes=[pltpu.VMEM((s, FG_W), F32), pltpu.VMEM((s, FG_W), F32)],
        compiler_params=pltpu.CompilerParams(
            dimension_semantics=("arbitrary", "arbitrary"), vmem_limit_bytes=VMEM_LIMIT),
        name="fourier",
    )(f, l1, m2, cs)


def _fft_tables(s):
    n1, n2 = FFT_N1, s // FFT_N1
    k1 = np.arange(n1, dtype=np.int64)
    s1 = np.arange(n1, dtype=np.int64)
    s2 = np.arange(n2, dtype=np.int64)
    num = (s2[:, None, None] * k1[None, :, None]
           + n2 * s1[None, None, :] * k1[None, :, None]) % s
    th = 2.0 * np.pi * num.astype(np.float64) / s
    l1 = np.concatenate([np.cos(th), -np.sin(th)], axis=1) / np.sqrt(n1)
    k2 = np.arange(n2, dtype=np.int64)
    th2 = 2.0 * np.pi * ((k2[:, None] * s2[None, :]) % n2).astype(np.float64) / n2
    c2, sn2 = np.cos(th2), np.sin(th2)
    m2 = np.block([[c2, sn2], [-sn2, c2]]) / np.sqrt(n2)
    c = np.arange(FG_W, dtype=np.int64)
    thc = 2.0 * np.pi * ((c[:, None] * c[None, :]) % FG_W).astype(np.float64) / FG_W
    cs = np.concatenate([np.cos(thc), np.sin(thc)], axis=0) / np.sqrt(FG_W)
    as_bf16 = lambda a: jnp.asarray(a.astype(np.float32)).astype(BF16)
    return as_bf16(l1), as_bf16(m2), as_bf16(cs)


def _mixout_kernel(x_ref, ao_ref, fo_ref, gain_ref, wga_ref, wgf_ref, wa_ref, wf_ref,
                   wo_ref, o_ref):
    x = x_ref[...]
    h = _rms_rows(x, gain_ref[...]).astype(BF16)
    ga = jax.nn.sigmoid(_dot(h, wga_ref[...]))
    gf = jax.nn.sigmoid(_dot(h, wgf_ref[...]))
    a = _dot(ao_ref[...], wa_ref[...])
    fb = _dot(fo_ref[...].astype(BF16), wf_ref[...])
    m = (ga * a + gf * fb).astype(BF16)
    o_ref[...] = x + _dot(m, wo_ref[...])


def _mixout(x, ao, fo, gain, w_gate_a, w_gate_f, w_attn_br, w_four_br, w_out, *, tm=512):
    t = x.shape[0]
    row = lambda w: pl.BlockSpec((tm, w), lambda i: (i, 0))
    return pl.pallas_call(
        _mixout_kernel,
        out_shape=jax.ShapeDtypeStruct(x.shape, F32),
        grid=(t // tm,),
        in_specs=[
            row(D_MODEL), row(ATT_W), row(FOUR_W),
            _resident((1, D_MODEL)),
            _resident((D_MODEL, D_MODEL)), _resident((D_MODEL, D_MODEL)),
            _resident((ATT_W, D_MODEL)), _resident((FOUR_W, D_MODEL)),
            _resident((D_MODEL, D_MODEL)),
        ],
        out_specs=row(D_MODEL),
        compiler_params=pltpu.CompilerParams(
            dimension_semantics=("arbitrary",), vmem_limit_bytes=VMEM_LIMIT),
        name="mixout",
    )(x, ao, fo, gain, w_gate_a, w_gate_f, w_attn_br, w_four_br, w_out)


def _attn_bias():
    qi = np.arange(BLOCK)[:, None]
    si = np.arange(3 * BLOCK)[None, :]
    dist = np.abs(qi + BLOCK - si).astype(np.float32)
    slopes = np.power(2.0, -8.0 * (np.arange(N_HEADS) + 1) / N_HEADS).astype(np.float32)
    bias = -slopes[:, None, None] * dist[None]
    return jnp.asarray(np.where(dist[None] <= WINDOW, bias, np.float32(NEG)).astype(np.float32))


def _head_mean_matrix(width):
    idx = np.arange(width) // HEAD_DIM
    return jnp.asarray((idx[:, None] == idx[None, :]).astype(np.float32) / HEAD_DIM).astype(BF16)


def _trunk(x, p, consts):
    b, s, _ = x.shape
    t = b * s
    l1, m2, cs = consts["fft"][s]
    x = x.reshape(t, D_MODEL)
    for l in range(DEPTH):
        x = _ffn(x, p["ln_ffn1"][l], p["w_ffn1_in"][l], p["w_ffn1_out"][l])
        q, kx, vx, f = _inproj(x, p["ln_mix"][l], p["w_qkvf"][l], consts["ones_q"],
                               consts["ones_k"], p["q_gain"][l], p["k_gain"][l])
        ao = _attn(q.reshape(b, s, ATT_W), kx.reshape(b, s, 4 * KV_W),
                   vx.reshape(b, s, 2 * KV_W), consts["bias"], p["sink"][l])
        fo = _fourier(f.reshape(b, s, FOUR_W), l1, m2, cs)
        x = _mixout(x, ao.reshape(t, ATT_W), fo.reshape(t, FOUR_W), p["ln_mix"][l],
                    p["w_gate_a"][l], p["w_gate_f"][l], p["w_attn_br"][l],
                    p["w_four_br"][l], p["w_out"][l])
        x = _ffn(x, p["ln_ffn2"][l], p["w_ffn2_in"][l], p["w_ffn2_out"][l])
    return x.reshape(b, s, D_MODEL)


def kernel(x_prompt, x_sample, ln_ffn1, w_ffn1_in, w_ffn1_out, ln_mix, w_in, q_gain, k_gain,
           sink, w_attn_br, w_four_br, w_out, ln_ffn2, w_ffn2_in, w_ffn2_out):
    depth = ln_ffn1.shape[0]
    w_in_b = w_in.astype(BF16)
    gate0 = QKVF_W
    p = {
        "ln_ffn1": ln_ffn1.reshape(depth, 1, D_MODEL),
        "ln_mix": ln_mix.reshape(depth, 1, D_MODEL),
        "ln_ffn2": ln_ffn2.reshape(depth, 1, D_MODEL),
        "w_ffn1_in": w_ffn1_in.astype(BF16), "w_ffn1_out": w_ffn1_out.astype(BF16),
        "w_ffn2_in": w_ffn2_in.astype(BF16), "w_ffn2_out": w_ffn2_out.astype(BF16),
        "w_qkvf": w_in_b[:, :, :gate0],
        "w_gate_a": w_in_b[:, :, gate0:gate0 + D_MODEL],
        "w_gate_f": w_in_b[:, :, gate0 + D_MODEL:],
        "q_gain": jnp.tile(q_gain, (1, N_HEADS)).reshape(depth, 1, ATT_W),
        "k_gain": jnp.tile(k_gain, (1, N_KV)).reshape(depth, 1, KV_W),
        "sink": sink,
        "w_attn_br": w_attn_br.astype(BF16), "w_four_br": w_four_br.astype(BF16),
        "w_out": w_out.astype(BF16),
    }
    consts = {
        "bias": _attn_bias(),
        "ones_q": _head_mean_matrix(ATT_W),
        "ones_k": _head_mean_matrix(KV_W),
        "fft": {s: _fft_tables(s) for s in {x_prompt.shape[1], x_sample.shape[1]}},
    }
    return (_trunk(x_prompt, p, consts), _trunk(x_sample, p, consts))
```

```python
import functools

import numpy as np
import jax
import jax.numpy as jnp
from jax import lax
from jax.experimental import pallas as pl
from jax.experimental.pallas import tpu as pltpu

D_MODEL = 1024
DEPTH = 4
HEAD_DIM = 64
N_HEADS = 8
N_KV = 2
GROUP = N_HEADS // N_KV
ATT_W = N_HEADS * HEAD_DIM
KV_W = N_KV * HEAD_DIM
N_FG = 4
FG_W = 128
FOUR_W = N_FG * FG_W
QKVF_W = ATT_W + 2 * KV_W + FOUR_W
D_FF = 2816
WINDOW = 128
BLOCK = 128
EPS = 1e-6
NEG = -1e30

LANES = 128
LOG2E = 1.4426950408889634
Q_SCALE = HEAD_DIM ** -0.5 * LOG2E
VT_ONES = 16
VT_ROWS = HEAD_DIM + VT_ONES
FFT_N1 = 64
VMEM_LIMIT = 56 * 1024 * 1024

BF16 = jnp.bfloat16
F32 = jnp.float32


def _dot(a, b):
    return jnp.dot(a, b, preferred_element_type=F32)


def _rms_rows(x, gain):
    ms = jnp.mean(x * x, axis=-1, keepdims=True)
    return x * lax.rsqrt(ms + EPS) * gain


def _resident(shape):
    nd = len(shape)
    return pl.BlockSpec(shape, lambda *_: (0,) * nd, pipeline_mode=pl.Buffered(1))


def _ffn_kernel(x_ref, gain_ref, wg_ref, wu_ref, wo_ref, o_ref, *, chunk):
    x = x_ref[...]
    h = _rms_rows(x, gain_ref[...]).astype(BF16)
    acc = jnp.zeros(x.shape, F32)
    for c in range(D_FF // chunk):
        sl = slice(c * chunk, (c + 1) * chunk)
        g = _dot(h, wg_ref[:, sl])
        u = _dot(h, wu_ref[:, sl])
        a = (jax.nn.silu(g) * u).astype(BF16)
        acc = acc + _dot(a, wo_ref[sl, :])
    o_ref[...] = x + 0.5 * acc


def _ffn(x, gain, w_in, w_out, *, tm=512, chunk=256):
    t = x.shape[0]
    return pl.pallas_call(
        functools.partial(_ffn_kernel, chunk=chunk),
        out_shape=jax.ShapeDtypeStruct(x.shape, F32),
        grid=(t // tm,),
        in_specs=[
            pl.BlockSpec((tm, D_MODEL), lambda i: (i, 0)),
            _resident((1, D_MODEL)),
            pl.BlockSpec((D_MODEL, D_FF), lambda i: (0, 0), pipeline_mode=pl.Buffered(1)),
            pl.BlockSpec((D_MODEL, D_FF), lambda i: (0, 1), pipeline_mode=pl.Buffered(1)),
            _resident((D_FF, D_MODEL)),
        ],
        out_specs=pl.BlockSpec((tm, D_MODEL), lambda i: (i, 0)),
        compiler_params=pltpu.CompilerParams(
            dimension_semantics=("arbitrary",), vmem_limit_bytes=VMEM_LIMIT),
        name="ffn",
    )(x, gain, w_in, w_in, w_out)


def _head_rms(t, ones_ref, gain):
    sq = t * t
    hi = sq.astype(BF16)
    lo = (sq - hi.astype(F32)).astype(BF16)
    ms = _dot(hi, ones_ref[...]) + _dot(lo, ones_ref[...])
    return t * lax.rsqrt(ms + EPS) * gain


def _inproj_kernel(x_ref, gain_ref, w_ref, onesq_ref, onesk_ref, qg_ref, kg_ref,
                   q_ref, kx_ref, vt_ref, f_ref):
    x = x_ref[...]
    h = _rms_rows(x, gain_ref[...]).astype(BF16)
    z = _dot(h, w_ref[...])
    q = z[:, :ATT_W]
    k = z[:, ATT_W:ATT_W + KV_W]
    v = z[:, ATT_W + KV_W:ATT_W + 2 * KV_W]
    f_ref[...] = z[:, ATT_W + 2 * KV_W:]
    q_ref[...] = (_head_rms(q, onesq_ref, qg_ref[...]) * Q_SCALE).astype(BF16)

    kn = _head_rms(k, onesk_ref, kg_ref[...])
    lane = lax.broadcasted_iota(jnp.int32, kn.shape, 1)
    low = lane < HEAD_DIM
    ksw = pltpu.roll(kn, HEAD_DIM, axis=1)
    zero = jnp.zeros_like(kn)
    kx = jnp.concatenate([
        jnp.where(low, kn, zero),
        jnp.where(low, zero, ksw),
        jnp.where(low, ksw, zero),
        jnp.where(low, zero, kn),
    ], axis=1)
    kx_ref[...] = kx.astype(BF16)

    vt = v.T
    ones = jnp.ones((VT_ONES, vt.shape[1]), F32)
    vt_ref[...] = jnp.concatenate(
        [vt[:HEAD_DIM], ones, vt[HEAD_DIM:], ones], axis=0).astype(BF16)


def _inproj(x, gain, w_qkvf, ones_q, ones_k, q_gain, k_gain, *, tm=512):
    t = x.shape[0]
    row = lambda w: pl.BlockSpec((tm, w), lambda i: (i, 0))
    return pl.pallas_call(
        _inproj_kernel,
        out_shape=(
            jax.ShapeDtypeStruct((t, ATT_W), BF16),
            jax.ShapeDtypeStruct((t, 4 * KV_W), BF16),
            jax.ShapeDtypeStruct((N_KV * VT_ROWS, t), BF16),
            jax.ShapeDtypeStruct((t, FOUR_W), F32),
        ),
        grid=(t // tm,),
        in_specs=[
            row(D_MODEL),
            _resident((1, D_MODEL)),
            _resident((D_MODEL, QKVF_W)),
            _resident((ATT_W, ATT_W)),
            _resident((KV_W, KV_W)),
            _resident((1, ATT_W)),
            _resident((1, KV_W)),
        ],
        out_specs=(row(ATT_W), row(4 * KV_W),
                   pl.BlockSpec((N_KV * VT_ROWS, tm), lambda i: (0, i)), row(FOUR_W)),
        compiler_params=pltpu.CompilerParams(
            dimension_semantics=("arbitrary",), vmem_limit_bytes=VMEM_LIMIT),
        name="inproj",
    )(x, gain, w_qkvf, ones_q, ones_k, q_gain, k_gain)


def _attn_kernel(sink_ref, q_ref, kx_ref, kxp_ref, kxn_ref, vt_ref, vtp_ref, vtn_ref,
                 bias_ref, o_ref, kbuf, vbuf, *, tq, nblk):
    i = pl.program_id(1)
    nr = tq // BLOCK
    win = 3 * BLOCK
    kbuf[0:BLOCK] = kxp_ref[...]
    kbuf[BLOCK:BLOCK + tq] = kx_ref[...]
    kbuf[BLOCK + tq:] = kxn_ref[...]
    vbuf[:, 0:BLOCK] = vtp_ref[...]
    vbuf[:, BLOCK:BLOCK + tq] = vt_ref[...]
    vbuf[:, BLOCK + tq:] = vtn_ref[...]

    def scores(r, pair):
        g = pair // 2
        rows = slice(r * BLOCK, (r + 1) * BLOCK)
        qpair = jnp.concatenate(
            [q_ref[rows, (2 * g) * LANES:(2 * g + 1) * LANES],
             q_ref[rows, (2 * g + 1) * LANES:(2 * g + 2) * LANES]], axis=0)
        kw = kbuf[r * BLOCK:r * BLOCK + win, pair * LANES:(pair + 1) * LANES]
        return lax.dot_general(kw, qpair, (((1,), (1,)), ((), ())),
                               preferred_element_type=F32)

    items = [(r, pair) for r in range(nr) for pair in range(2 * N_KV)]
    s_next = scores(*items[0])
    parts = [None] * N_HEADS
    for n, (r, pair) in enumerate(items):
        s = s_next
        if n + 1 < len(items):
            s_next = scores(*items[n + 1])
        g, par = pair // 2, pair % 2
        if r == 0:
            edge = jnp.where(i == 0, 1, 0)
        elif r == nr - 1:
            edge = jnp.where(i * nr + r == nblk - 1, 2, 0)
        else:
            edge = 0
        s = s + bias_ref[edge * (2 * N_KV) + pair]
        sink = sink_ref[pair]
        m = jnp.maximum(jnp.max(s, axis=0, keepdims=True), sink)
        p = jnp.exp2(s - m).astype(BF16)
        vt = vbuf[g * VT_ROWS:(g + 1) * VT_ROWS, r * BLOCK:r * BLOCK + win]
        oe = _dot(vt, p)
        den = oe[HEAD_DIM:HEAD_DIM + 1] + jnp.exp2(sink - m)
        o = oe[:HEAD_DIM] * (1.0 / den)
        parts[GROUP * g + par] = o[:, :LANES]
        parts[GROUP * g + 2 + par] = o[:, LANES:]
        if pair == 2 * N_KV - 1:
            o_ref[r * BLOCK:(r + 1) * BLOCK, :] = jnp.concatenate(parts, axis=0).T.astype(BF16)


def _attn(q, kx, vt, bias, sink, *, tq=512):
    b, s, _ = q.shape
    nr = tq // BLOCK
    nblk = s // BLOCK
    assert nr >= 2 and s % tq == 0
    main = lambda w: pl.BlockSpec((None, tq, w), lambda bi, i: (bi, i, 0))
    prev = lambda w: pl.BlockSpec(
        (None, BLOCK, w), lambda bi, i: (bi, jnp.maximum(i * nr - 1, 0), 0))
    nxt = lambda w: pl.BlockSpec(
        (None, BLOCK, w), lambda bi, i: (bi, jnp.minimum((i + 1) * nr, nblk - 1), 0))
    vrows = N_KV * VT_ROWS
    return pl.pallas_call(
        functools.partial(_attn_kernel, tq=tq, nblk=nblk),
        out_shape=jax.ShapeDtypeStruct((b, s, ATT_W), BF16),
        grid=(b, s // tq),
        in_specs=[
            _resident(sink.shape),
            main(ATT_W),
            main(4 * KV_W), prev(4 * KV_W), nxt(4 * KV_W),
            pl.BlockSpec((vrows, tq), lambda bi, i: (0, bi * (s // tq) + i)),
            pl.BlockSpec((vrows, BLOCK),
                         lambda bi, i: (0, bi * nblk + jnp.maximum(i * nr - 1, 0))),
            pl.BlockSpec((vrows, BLOCK),
                         lambda bi, i: (0, bi * nblk + jnp.minimum((i + 1) * nr, nblk - 1))),
            _resident(bias.shape),
        ],
        out_specs=main(ATT_W),
        scratch_shapes=[
            pltpu.VMEM((tq + 2 * BLOCK, 4 * KV_W), BF16),
            pltpu.VMEM((vrows, tq + 2 * BLOCK), BF16),
        ],
        compiler_params=pltpu.CompilerParams(
            dimension_semantics=("arbitrary", "arbitrary"), vmem_limit_bytes=VMEM_LIMIT),
        name="attn",
    )(sink, q, kx, kx, kx, vt, vt, vt, bias)


def _fourier_kernel(f_ref, l1_ref, m2_ref, cs_ref, o_ref, yr_ref, yi_ref, *, n1, n2):
    def stage1(s2, carry):
        a = f_ref[pl.ds(s2, n1, stride=n2), :].astype(BF16)
        y = _dot(l1_ref[s2], a)
        off = pl.multiple_of(s2 * n1, n1)
        yr_ref[pl.ds(off, n1), :] = y[:n1]
        yi_ref[pl.ds(off, n1), :] = y[n1:]
        return carry

    lax.fori_loop(0, n2, stage1, 0, unroll=8)

    def stage2(kp, carry):
        ys = []
        for d in range(2):
            k1 = 2 * kp + d
            yr = yr_ref[pl.ds(k1, n2, stride=n1), :]
            yi = yi_ref[pl.ds(k1, n2, stride=n1), :]
            ys.append(jnp.concatenate([yr, yi], axis=0))
        y = jnp.concatenate(ys, axis=1).astype(BF16)
        x = _dot(m2_ref[...], y)
        xc = jnp.concatenate([
            jnp.concatenate([x[:n2, :LANES], x[n2:, :LANES]], axis=1),
            jnp.concatenate([x[:n2, LANES:], x[n2:, LANES:]], axis=1)], axis=0).astype(BF16)
        o = _dot(xc, cs_ref[...])
        o_ref[pl.ds(2 * kp, n2, stride=n1), :] = o[:n2]
        o_ref[pl.ds(2 * kp + 1, n2, stride=n1), :] = o[n2:]
        return carry

    lax.fori_loop(0, n1 // 2, stage2, 0, unroll=4)


def _fourier(f, l1, m2, cs):
    b, s, _ = f.shape
    n1, n2 = FFT_N1, s // FFT_N1
    blk = pl.BlockSpec((None, s, FG_W), lambda bi, g: (bi, 0, g))
    return pl.pallas_call(
        functools.partial(_fourier_kernel, n1=n1, n2=n2),
        out_shape=jax.ShapeDtypeStruct(f.shape, F32),
        grid=(b, N_FG),
        in_specs=[blk, _resident(l1.shape), _resident(m2.shape), _resident(cs.shape)],
        out_specs=blk,
        scratch_shapes=[pltpu.VMEM((s, FG_W), F32), pltpu.VMEM((s, FG_W), F32)],
        compiler_params=pltpu.CompilerParams(
            dimension_semantics=("arbitrary", "arbitrary"), vmem_limit_bytes=VMEM_LIMIT),
        name="fourier",
    )(f, l1, m2, cs)


def _fft_tables(s):
    n1, n2 = FFT_N1, s // FFT_N1
    k1 = np.arange(n1, dtype=np.int64)
    s1 = np.arange(n1, dtype=np.int64)
    s2 = np.arange(n2, dtype=np.int64)
    num = (s2[:, None, None] * k1[None, :, None]
           + n2 * s1[None, None, :] * k1[None, :, None]) % s
    th = 2.0 * np.pi * num.astype(np.float64) / s
    l1 = np.concatenate([np.cos(th), -np.sin(th)], axis=1) / np.sqrt(n1)
    k2 = np.arange(n2, dtype=np.int64)
    th2 = 2.0 * np.pi * ((k2[:, None] * s2[None, :]) % n2).astype(np.float64) / n2
    c2, sn2 = np.cos(th2), np.sin(th2)
    m2 = np.block([[c2, sn2], [-sn2, c2]]) / np.sqrt(n2)
    c = np.arange(FG_W, dtype=np.int64)
    thc = 2.0 * np.pi * ((c[:, None] * c[None, :]) % FG_W).astype(np.float64) / FG_W
    cs = np.concatenate([np.cos(thc), np.sin(thc)], axis=0) / np.sqrt(FG_W)
    as_bf16 = lambda a: jnp.asarray(a.astype(np.float32)).astype(BF16)
    return as_bf16(l1), as_bf16(m2), as_bf16(cs)


def _mixout_kernel(x_ref, ao_ref, fo_ref, gain_ref, wga_ref, wgf_ref, wa_ref, wf_ref,
                   wo_ref, o_ref):
    x = x_ref[...]
    h = _rms_rows(x, gain_ref[...]).astype(BF16)
    ga = jax.nn.sigmoid(_dot(h, wga_ref[...]))
    gf = jax.nn.sigmoid(_dot(h, wgf_ref[...]))
    a = _dot(ao_ref[...], wa_ref[...])
    fb = _dot(fo_ref[...].astype(BF16), wf_ref[...])
    m = (ga * a + gf * fb).astype(BF16)
    o_ref[...] = x + _dot(m, wo_ref[...])


def _mixout(x, ao, fo, gain, w_gate_a, w_gate_f, w_attn_br, w_four_br, w_out, *, tm=512):
    t = x.shape[0]
    row = lambda w: pl.BlockSpec((tm, w), lambda i: (i, 0))
    return pl.pallas_call(
        _mixout_kernel,
        out_shape=jax.ShapeDtypeStruct(x.shape, F32),
        grid=(t // tm,),
        in_specs=[
            row(D_MODEL), row(ATT_W), row(FOUR_W),
            _resident((1, D_MODEL)),
            _resident((D_MODEL, D_MODEL)), _resident((D_MODEL, D_MODEL)),
            _resident((ATT_W, D_MODEL)), _resident((FOUR_W, D_MODEL)),
            _resident((D_MODEL, D_MODEL)),
        ],
        out_specs=row(D_MODEL),
        compiler_params=pltpu.CompilerParams(
            dimension_semantics=("arbitrary",), vmem_limit_bytes=VMEM_LIMIT),
        name="mixout",
    )(x, ao, fo, gain, w_gate_a, w_gate_f, w_attn_br, w_four_br, w_out)


def _attn_bias():
    si = np.arange(3 * BLOCK)[:, None]
    qi = np.arange(BLOCK)[None, :]
    dist = np.abs(qi + BLOCK - si).astype(np.float64)
    slopes = np.power(2.0, -8.0 * (np.arange(N_HEADS) + 1) / N_HEADS)
    out = np.empty((3, 2 * N_KV, 3 * BLOCK, 2 * LANES), np.float32)
    for edge in range(3):
        ok = dist <= WINDOW
        if edge == 1:
            ok = ok & (si >= BLOCK)
        if edge == 2:
            ok = ok & (si < 2 * BLOCK)
        for g in range(N_KV):
            for par in range(2):
                for half, h in enumerate((GROUP * g + par, GROUP * g + 2 + par)):
                    out[edge, 2 * g + par, :, half * LANES:(half + 1) * LANES] = np.where(
                        ok, -slopes[h] * dist * LOG2E, NEG)
    return jnp.asarray(out.reshape(3 * 2 * N_KV, 3 * BLOCK, 2 * LANES))


def _sink_table(sink):
    depth = sink.shape[0]
    s4 = sink.reshape(depth, N_KV, 2, 2) * LOG2E
    pairs = jnp.transpose(s4, (0, 1, 3, 2))
    return jnp.repeat(pairs.reshape(depth, 2 * N_KV, 1, 2), LANES, axis=-1)


def _head_mean_matrix(width):
    idx = np.arange(width) // HEAD_DIM
    return jnp.asarray((idx[:, None] == idx[None, :]).astype(np.float32) / HEAD_DIM).astype(BF16)


def _trunk(x, p, consts):
    b, s, _ = x.shape
    t = b * s
    l1, m2, cs = consts["fft"][s]
    x = x.reshape(t, D_MODEL)
    for l in range(DEPTH):
        x = _ffn(x, p["ln_ffn1"][l], p["w_ffn1_in"][l], p["w_ffn1_out"][l])
        q, kx, vt, f = _inproj(x, p["ln_mix"][l], p["w_qkvf"][l], consts["ones_q"],
                               consts["ones_k"], p["q_gain"][l], p["k_gain"][l])
        ao = _attn(q.reshape(b, s, ATT_W), kx.reshape(b, s, 4 * KV_W), vt,
                   consts["bias"], p["sink"][l])
        fo = _fourier(f.reshape(b, s, FOUR_W), l1, m2, cs)
        x = _mixout(x, ao.reshape(t, ATT_W), fo.reshape(t, FOUR_W), p["ln_mix"][l],
                    p["w_gate_a"][l], p["w_gate_f"][l], p["w_attn_br"][l],
                    p["w_four_br"][l], p["w_out"][l])
        x = _ffn(x, p["ln_ffn2"][l], p["w_ffn2_in"][l], p["w_ffn2_out"][l])
    return x.reshape(b, s, D_MODEL)


def kernel(x_prompt, x_sample, ln_ffn1, w_ffn1_in, w_ffn1_out, ln_mix, w_in, q_gain, k_gain,
           sink, w_attn_br, w_four_br, w_out, ln_ffn2, w_ffn2_in, w_ffn2_out):
    depth = ln_ffn1.shape[0]
    w_in_b = w_in.astype(BF16)
    gate0 = QKVF_W
    p = {
        "ln_ffn1": ln_ffn1.reshape(depth, 1, D_MODEL),
        "ln_mix": ln_mix.reshape(depth, 1, D_MODEL),
        "ln_ffn2": ln_ffn2.reshape(depth, 1, D_MODEL),
        "w_ffn1_in": w_ffn1_in.astype(BF16), "w_ffn1_out": w_ffn1_out.astype(BF16),
        "w_ffn2_in": w_ffn2_in.astype(BF16), "w_ffn2_out": w_ffn2_out.astype(BF16),
        "w_qkvf": w_in_b[:, :, :gate0],
        "w_gate_a": w_in_b[:, :, gate0:gate0 + D_MODEL],
        "w_gate_f": w_in_b[:, :, gate0 + D_MODEL:],
        "q_gain": jnp.tile(q_gain, (1, N_HEADS)).reshape(depth, 1, ATT_W),
        "k_gain": jnp.tile(k_gain, (1, N_KV)).reshape(depth, 1, KV_W),
        "sink": _sink_table(sink),
        "w_attn_br": w_attn_br.astype(BF16), "w_four_br": w_four_br.astype(BF16),
        "w_out": w_out.astype(BF16),
    }
    consts = {
        "bias": _attn_bias(),
        "ones_q": _head_mean_matrix(ATT_W),
        "ones_k": _head_mean_matrix(KV_W),
        "fft": {s: _fft_tables(s) for s in {x_prompt.shape[1], x_sample.shape[1]}},
    }
    return (_trunk(x_prompt, p, consts), _trunk(x_sample, p, consts))
```

```python
import functools

import numpy as np
import jax
import jax.numpy as jnp
from jax import lax
from jax.experimental import pallas as pl
from jax.experimental.pallas import tpu as pltpu

D_MODEL = 1024
DEPTH = 4
HEAD_DIM = 64
N_HEADS = 8
N_KV = 2
GROUP = N_HEADS // N_KV
ATT_W = N_HEADS * HEAD_DIM
KV_W = N_KV * HEAD_DIM
N_FG = 4
FG_W = 128
FOUR_W = N_FG * FG_W
QKVF_W = ATT_W + 2 * KV_W + FOUR_W
D_FF = 2816
WINDOW = 128
BLOCK = 128
EPS = 1e-6
NEG = -1e30

LANES = 128
LOG2E = 1.4426950408889634
Q_SCALE = HEAD_DIM ** -0.5 * LOG2E
VT_ONES = 16
VT_ROWS = HEAD_DIM + VT_ONES
FFT_N1 = 64
VMEM_LIMIT = 56 * 1024 * 1024

BF16 = jnp.bfloat16
F32 = jnp.float32


def _dot(a, b):
    return jnp.dot(a, b, preferred_element_type=F32)


def _rms_rows(x, gain):
    ms = jnp.mean(x * x, axis=-1, keepdims=True)
    return x * lax.rsqrt(ms + EPS) * gain


def _resident(shape):
    nd = len(shape)
    return pl.BlockSpec(shape, lambda *_: (0,) * nd, pipeline_mode=pl.Buffered(1))


def _ffn_kernel(x_ref, gain_ref, wg_ref, wu_ref, wo_ref, o_ref, *, chunk):
    x = x_ref[...]
    h = _rms_rows(x, gain_ref[...]).astype(BF16)
    acc = jnp.zeros(x.shape, F32)
    for c in range(D_FF // chunk):
        sl = slice(c * chunk, (c + 1) * chunk)
        g = _dot(h, wg_ref[:, sl])
        u = _dot(h, wu_ref[:, sl])
        a = (jax.nn.silu(g) * u).astype(BF16)
        acc = acc + _dot(a, wo_ref[sl, :])
    o_ref[...] = x + 0.5 * acc


def _ffn(x, gain, w_in, w_out, *, tm=1024, chunk=256):
    t = x.shape[0]
    return pl.pallas_call(
        functools.partial(_ffn_kernel, chunk=chunk),
        out_shape=jax.ShapeDtypeStruct(x.shape, F32),
        grid=(t // tm,),
        in_specs=[
            pl.BlockSpec((tm, D_MODEL), lambda i: (i, 0)),
            _resident((1, D_MODEL)),
            pl.BlockSpec((D_MODEL, D_FF), lambda i: (0, 0), pipeline_mode=pl.Buffered(1)),
            pl.BlockSpec((D_MODEL, D_FF), lambda i: (0, 1), pipeline_mode=pl.Buffered(1)),
            _resident((D_FF, D_MODEL)),
        ],
        out_specs=pl.BlockSpec((tm, D_MODEL), lambda i: (i, 0)),
        compiler_params=pltpu.CompilerParams(
            dimension_semantics=("arbitrary",), vmem_limit_bytes=VMEM_LIMIT),
        name="ffn",
    )(x, gain, w_in, w_in, w_out)


def _head_rms(t, ones_ref, gain):
    sq = t * t
    hi = sq.astype(BF16)
    lo = (sq - hi.astype(F32)).astype(BF16)
    ms = _dot(hi, ones_ref[...]) + _dot(lo, ones_ref[...])
    return t * lax.rsqrt(ms + EPS) * gain


def _inproj_kernel(x_ref, gain_ref, w_ref, onesq_ref, onesk_ref, qg_ref, kg_ref,
                   q_ref, kx_ref, vt_ref, f_ref):
    x = x_ref[...]
    h = _rms_rows(x, gain_ref[...]).astype(BF16)
    z = _dot(h, w_ref[...])
    q = z[:, :ATT_W]
    k = z[:, ATT_W:ATT_W + KV_W]
    v = z[:, ATT_W + KV_W:ATT_W + 2 * KV_W]
    f_ref[...] = z[:, ATT_W + 2 * KV_W:]
    q_ref[...] = (_head_rms(q, onesq_ref, qg_ref[...]) * Q_SCALE).astype(BF16)

    kn = _head_rms(k, onesk_ref, kg_ref[...])
    lane = lax.broadcasted_iota(jnp.int32, kn.shape, 1)
    low = lane < HEAD_DIM
    ksw = pltpu.roll(kn, HEAD_DIM, axis=1)
    zero = jnp.zeros_like(kn)
    kx = jnp.concatenate([
        jnp.where(low, kn, zero),
        jnp.where(low, zero, ksw),
        jnp.where(low, ksw, zero),
        jnp.where(low, zero, kn),
    ], axis=1)
    kx_ref[...] = kx.astype(BF16)

    vt = v.T
    ones = jnp.ones((VT_ONES, vt.shape[1]), F32)
    vt_ref[...] = jnp.concatenate(
        [vt[:HEAD_DIM], ones, vt[HEAD_DIM:], ones], axis=0).astype(BF16)


def _inproj(x, gain, w_qkvf, ones_q, ones_k, q_gain, k_gain, *, tm=512):
    t = x.shape[0]
    row = lambda w: pl.BlockSpec((tm, w), lambda i: (i, 0))
    return pl.pallas_call(
        _inproj_kernel,
        out_shape=(
            jax.ShapeDtypeStruct((t, ATT_W), BF16),
            jax.ShapeDtypeStruct((t, 4 * KV_W), BF16),
            jax.ShapeDtypeStruct((N_KV * VT_ROWS, t), BF16),
            jax.ShapeDtypeStruct((t, FOUR_W), F32),
        ),
        grid=(t // tm,),
        in_specs=[
            row(D_MODEL),
            _resident((1, D_MODEL)),
            _resident((D_MODEL, QKVF_W)),
            _resident((ATT_W, ATT_W)),
            _resident((KV_W, KV_W)),
            _resident((1, ATT_W)),
            _resident((1, KV_W)),
        ],
        out_specs=(row(ATT_W), row(4 * KV_W),
                   pl.BlockSpec((N_KV * VT_ROWS, tm), lambda i: (0, i)), row(FOUR_W)),
        compiler_params=pltpu.CompilerParams(
            dimension_semantics=("arbitrary",), vmem_limit_bytes=VMEM_LIMIT),
        name="inproj",
    )(x, gain, w_qkvf, ones_q, ones_k, q_gain, k_gain)


def _attn_kernel(sink_ref, q_ref, kx_ref, kxp_ref, kxn_ref, vt_ref, vtp_ref, vtn_ref,
                 bias_ref, o_ref, kbuf, vbuf, sbuf, *, tq, nblk):
    i = pl.program_id(1)
    nr = tq // BLOCK
    win = 3 * BLOCK
    kbuf[0:BLOCK] = kxp_ref[...]
    kbuf[BLOCK:BLOCK + tq] = kx_ref[...]
    kbuf[BLOCK + tq:] = kxn_ref[...]
    vbuf[:, 0:BLOCK] = vtp_ref[...]
    vbuf[:, BLOCK:BLOCK + tq] = vt_ref[...]
    vbuf[:, BLOCK + tq:] = vtn_ref[...]

    def scores(r, pair):
        g = pair // 2
        rows = slice(r * BLOCK, (r + 1) * BLOCK)
        qpair = jnp.concatenate(
            [q_ref[rows, (2 * g) * LANES:(2 * g + 1) * LANES],
             q_ref[rows, (2 * g + 1) * LANES:(2 * g + 2) * LANES]], axis=0)
        kw = kbuf[r * BLOCK:r * BLOCK + win, pair * LANES:(pair + 1) * LANES]
        return lax.dot_general(kw, qpair, (((1,), (1,)), ((), ())),
                               preferred_element_type=F32)

    items = [(r, pair) for r in range(nr) for pair in range(2 * N_KV)]
    nbuf = sbuf.shape[0]
    for n in range(nbuf - 1):
        sbuf[n] = scores(*items[n])
    parts = [None] * N_HEADS
    for n, (r, pair) in enumerate(items):
        ahead = n + nbuf - 1
        if ahead < len(items):
            sbuf[ahead % nbuf] = scores(*items[ahead])
        s = sbuf[n % nbuf]
        g, par = pair // 2, pair % 2
        if r == 0:
            edge = jnp.where(i == 0, 1, 0)
        elif r == nr - 1:
            edge = jnp.where(i * nr + r == nblk - 1, 2, 0)
        else:
            edge = 0
        s = s + bias_ref[edge * (2 * N_KV) + pair]
        sink = sink_ref[pair]
        m = jnp.maximum(jnp.max(s, axis=0, keepdims=True), sink)
        p = jnp.exp2(s - m).astype(BF16)
        vt = vbuf[g * VT_ROWS:(g + 1) * VT_ROWS, r * BLOCK:r * BLOCK + win]
        oe = _dot(vt, p)
        den = oe[HEAD_DIM:HEAD_DIM + 1] + jnp.exp2(sink - m)
        o = oe[:HEAD_DIM] * (1.0 / den)
        parts[GROUP * g + par] = o[:, :LANES]
        parts[GROUP * g + 2 + par] = o[:, LANES:]
        if pair == 2 * N_KV - 1:
            o_ref[r * BLOCK:(r + 1) * BLOCK, :] = jnp.concatenate(parts, axis=0).T.astype(BF16)


def _attn(q, kx, vt, bias, sink, *, tq=1024):
    b, s, _ = q.shape
    nr = tq // BLOCK
    nblk = s // BLOCK
    assert nr >= 2 and s % tq == 0
    main = lambda w: pl.BlockSpec((None, tq, w), lambda bi, i: (bi, i, 0))
    prev = lambda w: pl.BlockSpec(
        (None, BLOCK, w), lambda bi, i: (bi, jnp.maximum(i * nr - 1, 0), 0))
    nxt = lambda w: pl.BlockSpec(
        (None, BLOCK, w), lambda bi, i: (bi, jnp.minimum((i + 1) * nr, nblk - 1), 0))
    vrows = N_KV * VT_ROWS
    return pl.pallas_call(
        functools.partial(_attn_kernel, tq=tq, nblk=nblk),
        out_shape=jax.ShapeDtypeStruct((b, s, ATT_W), BF16),
        grid=(b, s // tq),
        in_specs=[
            _resident(sink.shape),
            main(ATT_W),
            main(4 * KV_W), prev(4 * KV_W), nxt(4 * KV_W),
            pl.BlockSpec((vrows, tq), lambda bi, i: (0, bi * (s // tq) + i)),
            pl.BlockSpec((vrows, BLOCK),
                         lambda bi, i: (0, bi * nblk + jnp.maximum(i * nr - 1, 0))),
            pl.BlockSpec((vrows, BLOCK),
                         lambda bi, i: (0, bi * nblk + jnp.minimum((i + 1) * nr, nblk - 1))),
            _resident(bias.shape),
        ],
        out_specs=main(ATT_W),
        scratch_shapes=[
            pltpu.VMEM((tq + 2 * BLOCK, 4 * KV_W), BF16),
            pltpu.VMEM((vrows, tq + 2 * BLOCK), BF16),
            pltpu.VMEM((6, 3 * BLOCK, 2 * LANES), F32),
        ],
        compiler_params=pltpu.CompilerParams(
            dimension_semantics=("arbitrary", "arbitrary"), vmem_limit_bytes=VMEM_LIMIT),
        name="attn",
    )(sink, q, kx, kx, kx, vt, vt, vt, bias)


def _fourier_kernel(f_ref, l1_ref, m2_ref, cs_ref, o_ref, yr_ref, yi_ref, *, n1, n2):
    def stage1(s2, carry):
        a = f_ref[pl.ds(s2, n1, stride=n2), :].astype(BF16)
        y = _dot(l1_ref[s2], a)
        off = pl.multiple_of(s2 * n1, n1)
        yr_ref[pl.ds(off, n1), :] = y[:n1]
        yi_ref[pl.ds(off, n1), :] = y[n1:]
        return carry

    lax.fori_loop(0, n2, stage1, 0, unroll=8)

    def stage2(kp, carry):
        ys = []
        for d in range(2):
            k1 = 2 * kp + d
            yr = yr_ref[pl.ds(k1, n2, stride=n1), :]
            yi = yi_ref[pl.ds(k1, n2, stride=n1), :]
            ys.append(jnp.concatenate([yr, yi], axis=0))
        y = jnp.concatenate(ys, axis=1).astype(BF16)
        x = _dot(m2_ref[...], y)
        xc = jnp.concatenate([
            jnp.concatenate([x[:n2, :LANES], x[n2:, :LANES]], axis=1),
            jnp.concatenate([x[:n2, LANES:], x[n2:, LANES:]], axis=1)], axis=0).astype(BF16)
        o = _dot(xc, cs_ref[...])
        o_ref[pl.ds(2 * kp, n2, stride=n1), :] = o[:n2]
        o_ref[pl.ds(2 * kp + 1, n2, stride=n1), :] = o[n2:]
        return carry

    lax.fori_loop(0, n1 // 2, stage2, 0, unroll=4)


def _fourier(f, l1, m2, cs):
    b, s, _ = f.shape
    n1, n2 = FFT_N1, s // FFT_N1
    blk = pl.BlockSpec((None, s, FG_W), lambda bi, g: (bi, 0, g))
    return pl.pallas_call(
        functools.partial(_fourier_kernel, n1=n1, n2=n2),
        out_shape=jax.ShapeDtypeStruct(f.shape, F32),
        grid=(b, N_FG),
        in_specs=[blk, _resident(l1.shape), _resident(m2.shape), _resident(cs.shape)],
        out_specs=blk,
        scratch_shapes=[pltpu.VMEM((s, FG_W), F32), pltpu.VMEM((s, FG_W), F32)],
        compiler_params=pltpu.CompilerParams(
            dimension_semantics=("arbitrary", "arbitrary"), vmem_limit_bytes=VMEM_LIMIT),
        name="fourier",
    )(f, l1, m2, cs)


def _fft_tables(s):
    n1, n2 = FFT_N1, s // FFT_N1
    k1 = np.arange(n1, dtype=np.int64)
    s1 = np.arange(n1, dtype=np.int64)
    s2 = np.arange(n2, dtype=np.int64)
    num = (s2[:, None, None] * k1[None, :, None]
           + n2 * s1[None, None, :] * k1[None, :, None]) % s
    th = 2.0 * np.pi * num.astype(np.float64) / s
    l1 = np.concatenate([np.cos(th), -np.sin(th)], axis=1) / np.sqrt(n1)
    k2 = np.arange(n2, dtype=np.int64)
    th2 = 2.0 * np.pi * ((k2[:, None] * s2[None, :]) % n2).astype(np.float64) / n2
    c2, sn2 = np.cos(th2), np.sin(th2)
    m2 = np.block([[c2, sn2], [-sn2, c2]]) / np.sqrt(n2)
    c = np.arange(FG_W, dtype=np.int64)
    thc = 2.0 * np.pi * ((c[:, None] * c[None, :]) % FG_W).astype(np.float64) / FG_W
    cs = np.concatenate([np.cos(thc), np.sin(thc)], axis=0) / np.sqrt(FG_W)
    as_bf16 = lambda a: jnp.asarray(a.astype(np.float32)).astype(BF16)
    return as_bf16(l1), as_bf16(m2), as_bf16(cs)


def _mixout_kernel(x_ref, ao_ref, fo_ref, gain_ref, wga_ref, wgf_ref, wa_ref, wf_ref,
                   wo_ref, o_ref):
    x = x_ref[...]
    h = _rms_rows(x, gain_ref[...]).astype(BF16)
    ga = jax.nn.sigmoid(_dot(h, wga_ref[...]))
    gf = jax.nn.sigmoid(_dot(h, wgf_ref[...]))
    a = _dot(ao_ref[...], wa_ref[...])
    fb = _dot(fo_ref[...].astype(BF16), wf_ref[...])
    m = (ga * a + gf * fb).astype(BF16)
    o_ref[...] = x + _dot(m, wo_ref[...])


def _mixout(x, ao, fo, gain, w_gate_a, w_gate_f, w_attn_br, w_four_br, w_out, *, tm=1024):
    t = x.shape[0]
    row = lambda w: pl.BlockSpec((tm, w), lambda i: (i, 0))
    return pl.pallas_call(
        _mixout_kernel,
        out_shape=jax.ShapeDtypeStruct(x.shape, F32),
        grid=(t // tm,),
        in_specs=[
            row(D_MODEL), row(ATT_W), row(FOUR_W),
            _resident((1, D_MODEL)),
            _resident((D_MODEL, D_MODEL)), _resident((D_MODEL, D_MODEL)),
            _resident((ATT_W, D_MODEL)), _resident((FOUR_W, D_MODEL)),
            _resident((D_MODEL, D_MODEL)),
        ],
        out_specs=row(D_MODEL),
        compiler_params=pltpu.CompilerParams(
            dimension_semantics=("arbitrary",), vmem_limit_bytes=VMEM_LIMIT),
        name="mixout",
    )(x, ao, fo, gain, w_gate_a, w_gate_f, w_attn_br, w_four_br, w_out)


def _attn_bias():
    si = np.arange(3 * BLOCK)[:, None]
    qi = np.arange(BLOCK)[None, :]
    dist = np.abs(qi + BLOCK - si).astype(np.float64)
    slopes = np.power(2.0, -8.0 * (np.arange(N_HEADS) + 1) / N_HEADS)
    out = np.empty((3, 2 * N_KV, 3 * BLOCK, 2 * LANES), np.float32)
    for edge in range(3):
        ok = dist <= WINDOW
        if edge == 1:
            ok = ok & (si >= BLOCK)
        if edge == 2:
            ok = ok & (si < 2 * BLOCK)
        for g in range(N_KV):
            for par in range(2):
                for half, h in enumerate((GROUP * g + par, GROUP * g + 2 + par)):
                    out[edge, 2 * g + par, :, half * LANES:(half + 1) * LANES] = np.where(
                        ok, -slopes[h] * dist * LOG2E, NEG)
    return jnp.asarray(out.reshape(3 * 2 * N_KV, 3 * BLOCK, 2 * LANES))


def _sink_table(sink):
    depth = sink.shape[0]
    s4 = sink.reshape(depth, N_KV, 2, 2) * LOG2E
    pairs = jnp.transpose(s4, (0, 1, 3, 2))
    return jnp.repeat(pairs.reshape(depth, 2 * N_KV, 1, 2), LANES, axis=-1)


def _head_mean_matrix(width):
    idx = np.arange(width) // HEAD_DIM
    return jnp.asarray((idx[:, None] == idx[None, :]).astype(np.float32) / HEAD_DIM).astype(BF16)


def _trunk(x, p, consts):
    b, s, _ = x.shape
    t = b * s
    l1, m2, cs = consts["fft"][s]
    x = x.reshape(t, D_MODEL)
    for l in range(DEPTH):
        x = _ffn(x, p["ln_ffn1"][l], p["w_ffn1_in"][l], p["w_ffn1_out"][l])
        q, kx, vt, f = _inproj(x, p["ln_mix"][l], p["w_qkvf"][l], consts["ones_q"],
                               consts["ones_k"], p["q_gain"][l], p["k_gain"][l])
        ao = _attn(q.reshape(b, s, ATT_W), kx.reshape(b, s, 4 * KV_W), vt,
                   consts["bias"], p["sink"][l])
        fo = _fourier(f.reshape(b, s, FOUR_W), l1, m2, cs)
        x = _mixout(x, ao.reshape(t, ATT_W), fo.reshape(t, FOUR_W), p["ln_mix"][l],
                    p["w_gate_a"][l], p["w_gate_f"][l], p["w_attn_br"][l],
                    p["w_four_br"][l], p["w_out"][l])
        x = _ffn(x, p["ln_ffn2"][l], p["w_ffn2_in"][l], p["w_ffn2_out"][l])
    return x.reshape(b, s, D_MODEL)


def kernel(x_prompt, x_sample, ln_ffn1, w_ffn1_in, w_ffn1_out, ln_mix, w_in, q_gain, k_gain,
           sink, w_attn_br, w_four_br, w_out, ln_ffn2, w_ffn2_in, w_ffn2_out):
    depth = ln_ffn1.shape[0]
    w_in_b = w_in.astype(BF16)
    gate0 = QKVF_W
    p = {
        "ln_ffn1": ln_ffn1.reshape(depth, 1, D_MODEL),
        "ln_mix": ln_mix.reshape(depth, 1, D_MODEL),
        "ln_ffn2": ln_ffn2.reshape(depth, 1, D_MODEL),
        "w_ffn1_in": w_ffn1_in.astype(BF16), "w_ffn1_out": w_ffn1_out.astype(BF16),
        "w_ffn2_in": w_ffn2_in.astype(BF16), "w_ffn2_out": w_ffn2_out.astype(BF16),
        "w_qkvf": w_in_b[:, :, :gate0],
        "w_gate_a": w_in_b[:, :, gate0:gate0 + D_MODEL],
        "w_gate_f": w_in_b[:, :, gate0 + D_MODEL:],
        "q_gain": jnp.tile(q_gain, (1, N_HEADS)).reshape(depth, 1, ATT_W),
        "k_gain": jnp.tile(k_gain, (1, N_KV)).reshape(depth, 1, KV_W),
        "sink": _sink_table(sink),
        "w_attn_br": w_attn_br.astype(BF16), "w_four_br": w_four_br.astype(BF16),
        "w_out": w_out.astype(BF16),
    }
    consts = {
        "bias": _attn_bias(),
        "ones_q": _head_mean_matrix(ATT_W),
        "ones_k": _head_mean_matrix(KV_W),
        "fft": {s: _fft_tables(s) for s in {x_prompt.shape[1], x_sample.shape[1]}},
    }
    return (_trunk(x_prompt, p, consts), _trunk(x_sample, p, consts))
```

```python
import functools

import numpy as np
import jax
import jax.numpy as jnp
from jax import lax
from jax.experimental import pallas as pl
from jax.experimental.pallas import tpu as pltpu

D_MODEL = 1024
DEPTH = 4
HEAD_DIM = 64
N_HEADS = 8
N_KV = 2
GROUP = N_HEADS // N_KV
ATT_W = N_HEADS * HEAD_DIM
KV_W = N_KV * HEAD_DIM
N_FG = 4
FG_W = 128
FOUR_W = N_FG * FG_W
QKVF_W = ATT_W + 2 * KV_W + FOUR_W
D_FF = 2816
WINDOW = 128
BLOCK = 128
EPS = 1e-6
NEG = -1e30

LANES = 128
LOG2E = 1.4426950408889634
Q_SCALE = HEAD_DIM ** -0.5 * LOG2E
VT_ONES = 16
VT_ROWS = HEAD_DIM + VT_ONES
FFT_N1 = 64
PITCH_PAD = 8
VMEM_LIMIT = 56 * 1024 * 1024

BF16 = jnp.bfloat16
F32 = jnp.float32


def _dot(a, b):
    return jnp.dot(a, b, preferred_element_type=F32)


def _rms_rows(x, gain):
    ms = jnp.mean(x * x, axis=-1, keepdims=True)
    return x * lax.rsqrt(ms + EPS) * gain


def _resident(shape):
    nd = len(shape)
    return pl.BlockSpec(shape, lambda *_: (0,) * nd, pipeline_mode=pl.Buffered(1))


def _ffn_kernel(x_ref, gain_ref, wg_ref, wu_ref, wo_ref, o_ref, *, chunk):
    x = x_ref[...]
    h = _rms_rows(x, gain_ref[...]).astype(BF16)
    acc = jnp.zeros(x.shape, F32)
    for c in range(D_FF // chunk):
        sl = slice(c * chunk, (c + 1) * chunk)
        g = _dot(h, wg_ref[:, sl])
        u = _dot(h, wu_ref[:, sl])
        a = (jax.nn.silu(g) * u).astype(BF16)
        acc = acc + _dot(a, wo_ref[sl, :])
    o_ref[...] = x + 0.5 * acc


def _ffn(x, gain, w_in, w_out, *, tm=1024, chunk=256):
    t = x.shape[0]
    return pl.pallas_call(
        functools.partial(_ffn_kernel, chunk=chunk),
        out_shape=jax.ShapeDtypeStruct(x.shape, F32),
        grid=(t // tm,),
        in_specs=[
            pl.BlockSpec((tm, D_MODEL), lambda i: (i, 0)),
            _resident((1, D_MODEL)),
            pl.BlockSpec((D_MODEL, D_FF), lambda i: (0, 0), pipeline_mode=pl.Buffered(1)),
            pl.BlockSpec((D_MODEL, D_FF), lambda i: (0, 1), pipeline_mode=pl.Buffered(1)),
            _resident((D_FF, D_MODEL)),
        ],
        out_specs=pl.BlockSpec((tm, D_MODEL), lambda i: (i, 0)),
        compiler_params=pltpu.CompilerParams(
            dimension_semantics=("arbitrary",), vmem_limit_bytes=VMEM_LIMIT),
        name="ffn",
    )(x, gain, w_in, w_in, w_out)


def _head_rms(t, ones_ref, gain):
    sq = t * t
    hi = sq.astype(BF16)
    lo = (sq - hi.astype(F32)).astype(BF16)
    ones = ones_ref[...]
    width = ones.shape[0]
    ms = jnp.concatenate(
        [_dot(hi[:, c:c + width], ones) + _dot(lo[:, c:c + width], ones)
         for c in range(0, t.shape[1], width)], axis=1)
    return t * lax.rsqrt(ms + EPS) * gain


def _inproj_kernel(x_ref, gain_ref, w_ref, onesq_ref, onesk_ref, qg_ref, kg_ref,
                   q_ref, kx_ref, vt_ref, f_ref, *, n2):
    x = x_ref[...]
    h = _rms_rows(x, gain_ref[...]).astype(BF16)
    z = _dot(h, w_ref[...])
    q = z[:, :ATT_W]
    k = z[:, ATT_W:ATT_W + KV_W]
    v = z[:, ATT_W + KV_W:ATT_W + 2 * KV_W]
    f = z[:, ATT_W + 2 * KV_W:]
    p2 = n2 + PITCH_PAD
    for j in range(x.shape[0] // n2):
        f_ref[j * p2:j * p2 + n2, :] = f[j * n2:(j + 1) * n2]
        f_ref[j * p2 + n2:(j + 1) * p2, :] = jnp.zeros((PITCH_PAD, FOUR_W), F32)
    q_ref[...] = (_head_rms(q, onesq_ref, qg_ref[...]) * Q_SCALE).astype(BF16)

    kn = _head_rms(k, onesk_ref, kg_ref[...])
    lane = lax.broadcasted_iota(jnp.int32, kn.shape, 1)
    low = lane < HEAD_DIM
    ksw = pltpu.roll(kn, HEAD_DIM, axis=1)
    zero = jnp.zeros_like(kn)
    kx = jnp.concatenate([
        jnp.where(low, kn, zero),
        jnp.where(low, zero, ksw),
        jnp.where(low, ksw, zero),
        jnp.where(low, zero, kn),
    ], axis=1)
    kx_ref[...] = kx.astype(BF16)

    vt = v.T
    ones = jnp.ones((VT_ONES, vt.shape[1]), F32)
    vt_ref[...] = jnp.concatenate(
        [vt[:HEAD_DIM], ones, vt[HEAD_DIM:], ones], axis=0).astype(BF16)


def _inproj(x, gain, w_qkvf, ones_q, ones_k, q_gain, k_gain, *, n2, tm=512):
    t = x.shape[0]
    assert tm % n2 == 0
    row = lambda w: pl.BlockSpec((tm, w), lambda i: (i, 0))
    f_rows = tm // n2 * (n2 + PITCH_PAD)
    return pl.pallas_call(
        functools.partial(_inproj_kernel, n2=n2),
        out_shape=(
            jax.ShapeDtypeStruct((t, ATT_W), BF16),
            jax.ShapeDtypeStruct((t, 4 * KV_W), BF16),
            jax.ShapeDtypeStruct((N_KV * VT_ROWS, t), BF16),
            jax.ShapeDtypeStruct((t // tm * f_rows, FOUR_W), F32),
        ),
        grid=(t // tm,),
        in_specs=[
            row(D_MODEL),
            _resident((1, D_MODEL)),
            _resident((D_MODEL, QKVF_W)),
            _resident((2 * LANES, 2 * LANES)),
            _resident((KV_W, KV_W)),
            _resident((1, ATT_W)),
            _resident((1, KV_W)),
        ],
        out_specs=(row(ATT_W), row(4 * KV_W),
                   pl.BlockSpec((N_KV * VT_ROWS, tm), lambda i: (0, i)),
                   pl.BlockSpec((f_rows, FOUR_W), lambda i: (i, 0))),
        compiler_params=pltpu.CompilerParams(
            dimension_semantics=("arbitrary",), vmem_limit_bytes=VMEM_LIMIT),
        name="inproj",
    )(x, gain, w_qkvf, ones_q, ones_k, q_gain, k_gain)


def _attn_kernel(sink_ref, q_ref, kx_ref, kxp_ref, kxn_ref, vt_ref, vtp_ref, vtn_ref,
                 bias_ref, o_ref, kbuf, vbuf, sbuf, *, tq, nblk):
    i = pl.program_id(1)
    nr = tq // BLOCK
    win = 3 * BLOCK
    kbuf[0:BLOCK] = kxp_ref[...]
    kbuf[BLOCK:BLOCK + tq] = kx_ref[...]
    kbuf[BLOCK + tq:] = kxn_ref[...]
    vbuf[:, 0:BLOCK] = vtp_ref[...]
    vbuf[:, BLOCK:BLOCK + tq] = vt_ref[...]
    vbuf[:, BLOCK + tq:] = vtn_ref[...]

    def scores(r, pair):
        g = pair // 2
        rows = slice(r * BLOCK, (r + 1) * BLOCK)
        qpair = jnp.concatenate(
            [q_ref[rows, (2 * g) * LANES:(2 * g + 1) * LANES],
             q_ref[rows, (2 * g + 1) * LANES:(2 * g + 2) * LANES]], axis=0)
        kw = kbuf[r * BLOCK:r * BLOCK + win, pair * LANES:(pair + 1) * LANES]
        return lax.dot_general(kw, qpair, (((1,), (1,)), ((), ())),
                               preferred_element_type=F32)

    items = [(r, pair) for r in range(nr) for pair in range(2 * N_KV)]
    nbuf = sbuf.shape[0]
    for n in range(nbuf - 1):
        sbuf[n] = scores(*items[n])
    parts = [None] * N_HEADS
    for n, (r, pair) in enumerate(items):
        ahead = n + nbuf - 1
        if ahead < len(items):
            sbuf[ahead % nbuf] = scores(*items[ahead])
        s = sbuf[n % nbuf]
        g, par = pair // 2, pair % 2
        if r == 0:
            edge = jnp.where(i == 0, 1, 0)
        elif r == nr - 1:
            edge = jnp.where(i * nr + r == nblk - 1, 2, 0)
        else:
            edge = 0
        s = s + bias_ref[edge * (2 * N_KV) + pair]
        sink = sink_ref[pair]
        m = jnp.maximum(jnp.max(s, axis=0, keepdims=True), sink)
        p = jnp.exp2(s - m).astype(BF16)
        vt = vbuf[g * VT_ROWS:(g + 1) * VT_ROWS, r * BLOCK:r * BLOCK + win]
        oe = _dot(vt, p)
        den = oe[HEAD_DIM:HEAD_DIM + 1] + jnp.exp2(sink - m)
        o = oe[:HEAD_DIM] * (1.0 / den)
        parts[GROUP * g + par] = o[:, :LANES]
        parts[GROUP * g + 2 + par] = o[:, LANES:]
        if pair == 2 * N_KV - 1:
            o_ref[r * BLOCK:(r + 1) * BLOCK, :] = jnp.concatenate(parts, axis=0).T.astype(BF16)


def _attn(q, kx, vt, bias, sink, *, tq=1024):
    b, s, _ = q.shape
    nr = tq // BLOCK
    nblk = s // BLOCK
    assert nr >= 2 and s % tq == 0
    main = lambda w: pl.BlockSpec((None, tq, w), lambda bi, i: (bi, i, 0))
    prev = lambda w: pl.BlockSpec(
        (None, BLOCK, w), lambda bi, i: (bi, jnp.maximum(i * nr - 1, 0), 0))
    nxt = lambda w: pl.BlockSpec(
        (None, BLOCK, w), lambda bi, i: (bi, jnp.minimum((i + 1) * nr, nblk - 1), 0))
    vrows = N_KV * VT_ROWS
    return pl.pallas_call(
        functools.partial(_attn_kernel, tq=tq, nblk=nblk),
        out_shape=jax.ShapeDtypeStruct((b, s, ATT_W), BF16),
        grid=(b, s // tq),
        in_specs=[
            _resident(sink.shape),
            main(ATT_W),
            main(4 * KV_W), prev(4 * KV_W), nxt(4 * KV_W),
            pl.BlockSpec((vrows, tq), lambda bi, i: (0, bi * (s // tq) + i)),
            pl.BlockSpec((vrows, BLOCK),
                         lambda bi, i: (0, bi * nblk + jnp.maximum(i * nr - 1, 0))),
            pl.BlockSpec((vrows, BLOCK),
                         lambda bi, i: (0, bi * nblk + jnp.minimum((i + 1) * nr, nblk - 1))),
            _resident(bias.shape),
        ],
        out_specs=main(ATT_W),
        scratch_shapes=[
            pltpu.VMEM((tq + 2 * BLOCK, 4 * KV_W), BF16),
            pltpu.VMEM((vrows, tq + 2 * BLOCK), BF16),
            pltpu.VMEM((6, 3 * BLOCK, 2 * LANES), F32),
        ],
        compiler_params=pltpu.CompilerParams(
            dimension_semantics=("arbitrary", "arbitrary"), vmem_limit_bytes=VMEM_LIMIT),
        name="attn",
    )(sink, q, kx, kx, kx, vt, vt, vt, bias)


def _fourier_kernel(f_ref, l1_ref, m2_ref, cs_ref, o_ref, yr_ref, yi_ref, xbuf, *, n1, n2):
    p2 = n2 + PITCH_PAD
    p1 = n1 + PITCH_PAD
    for s2 in range(n2):
        a = f_ref[pl.ds(s2, n1, stride=p2), :].astype(BF16)
        y = _dot(l1_ref[s2], a)
        yr_ref[s2 * p1:s2 * p1 + n1, :] = y[:n1]
        yi_ref[s2 * p1:s2 * p1 + n1, :] = y[n1:]

    def stage2(kp):
        ys = [jnp.concatenate([yr_ref[pl.ds(2 * kp + d, n2, stride=p1), :],
                               yi_ref[pl.ds(2 * kp + d, n2, stride=p1), :]], axis=0)
              for d in range(2)]
        return _dot(m2_ref[...], jnp.concatenate(ys, axis=1).astype(BF16))

    nbuf = xbuf.shape[0]
    npairs = n1 // 2
    for kp in range(nbuf - 1):
        xbuf[kp] = stage2(kp)
    for kp in range(npairs):
        ahead = kp + nbuf - 1
        if ahead < npairs:
            xbuf[ahead % nbuf] = stage2(ahead)
        x = xbuf[kp % nbuf]
        xc = jnp.concatenate([
            jnp.concatenate([x[:n2, :LANES], x[n2:, :LANES]], axis=1),
            jnp.concatenate([x[:n2, LANES:], x[n2:, LANES:]], axis=1)], axis=0).astype(BF16)
        o = _dot(xc, cs_ref[...])
        o_ref[pl.ds(2 * kp, n2, stride=p1), :] = o[:n2]
        o_ref[pl.ds(2 * kp + 1, n2, stride=p1), :] = o[n2:]
    for k2 in range(n2):
        o_ref[k2 * p1 + n1:(k2 + 1) * p1, :] = jnp.zeros((PITCH_PAD, FG_W), F32)


def _fourier(f, l1, m2, cs):
    b, rows_in, _ = f.shape
    n1 = FFT_N1
    n2 = rows_in // n1 - PITCH_PAD
    rows_out = n2 * (n1 + PITCH_PAD)
    return pl.pallas_call(
        functools.partial(_fourier_kernel, n1=n1, n2=n2),
        out_shape=jax.ShapeDtypeStruct((b, rows_out, FOUR_W), F32),
        grid=(b, N_FG),
        in_specs=[pl.BlockSpec((None, rows_in, FG_W), lambda bi, g: (bi, 0, g)),
                  _resident(l1.shape), _resident(m2.shape), _resident(cs.shape)],
        out_specs=pl.BlockSpec((None, rows_out, FG_W), lambda bi, g: (bi, 0, g)),
        scratch_shapes=[pltpu.VMEM((rows_out, FG_W), F32), pltpu.VMEM((rows_out, FG_W), F32),
                        pltpu.VMEM((4, 2 * n2, 2 * LANES), F32)],
        compiler_params=pltpu.CompilerParams(
            dimension_semantics=("arbitrary", "arbitrary"), vmem_limit_bytes=VMEM_LIMIT),
        name="fourier",
    )(f, l1, m2, cs)


def _fft_tables(s):
    n1, n2 = FFT_N1, s // FFT_N1
    k1 = np.arange(n1, dtype=np.int64)
    s1 = np.arange(n1, dtype=np.int64)
    s2 = np.arange(n2, dtype=np.int64)
    num = (s2[:, None, None] * k1[None, :, None]
           + n2 * s1[None, None, :] * k1[None, :, None]) % s
    th = 2.0 * np.pi * num.astype(np.float64) / s
    l1 = np.concatenate([np.cos(th), -np.sin(th)], axis=1) / np.sqrt(n1)
    k2 = np.arange(n2, dtype=np.int64)
    th2 = 2.0 * np.pi * ((k2[:, None] * s2[None, :]) % n2).astype(np.float64) / n2
    c2, sn2 = np.cos(th2), np.sin(th2)
    m2 = np.block([[c2, sn2], [-sn2, c2]]) / np.sqrt(n2)
    c = np.arange(FG_W, dtype=np.int64)
    thc = 2.0 * np.pi * ((c[:, None] * c[None, :]) % FG_W).astype(np.float64) / FG_W
    cs = np.concatenate([np.cos(thc), np.sin(thc)], axis=0) / np.sqrt(FG_W)
    as_bf16 = lambda a: jnp.asarray(a.astype(np.float32)).astype(BF16)
    return as_bf16(l1), as_bf16(m2), as_bf16(cs)


def _mixout_kernel(x_ref, ao_ref, fo_ref, gain_ref, wga_ref, wgf_ref, wa_ref, wf_ref,
                   wo_ref, o_ref):
    x = x_ref[...]
    h = _rms_rows(x, gain_ref[...]).astype(BF16)
    ga = jax.nn.sigmoid(_dot(h, wga_ref[...]))
    gf = jax.nn.sigmoid(_dot(h, wgf_ref[...]))
    a = _dot(ao_ref[...], wa_ref[...])
    p1 = FFT_N1 + PITCH_PAD
    fo = jnp.concatenate([fo_ref[j * p1:j * p1 + FFT_N1, :]
                          for j in range(x.shape[0] // FFT_N1)], axis=0)
    fb = _dot(fo.astype(BF16), wf_ref[...])
    m = (ga * a + gf * fb).astype(BF16)
    o_ref[...] = x + _dot(m, wo_ref[...])


def _mixout(x, ao, fo, gain, w_gate_a, w_gate_f, w_attn_br, w_four_br, w_out, *, tm=1024):
    t = x.shape[0]
    row = lambda w: pl.BlockSpec((tm, w), lambda i: (i, 0))
    fo_rows = tm // FFT_N1 * (FFT_N1 + PITCH_PAD)
    return pl.pallas_call(
        _mixout_kernel,
        out_shape=jax.ShapeDtypeStruct(x.shape, F32),
        grid=(t // tm,),
        in_specs=[
            row(D_MODEL), row(ATT_W), pl.BlockSpec((fo_rows, FOUR_W), lambda i: (i, 0)),
            _resident((1, D_MODEL)),
            _resident((D_MODEL, D_MODEL)), _resident((D_MODEL, D_MODEL)),
            _resident((ATT_W, D_MODEL)), _resident((FOUR_W, D_MODEL)),
            _resident((D_MODEL, D_MODEL)),
        ],
        out_specs=row(D_MODEL),
        compiler_params=pltpu.CompilerParams(
            dimension_semantics=("arbitrary",), vmem_limit_bytes=VMEM_LIMIT),
        name="mixout",
    )(x, ao, fo, gain, w_gate_a, w_gate_f, w_attn_br, w_four_br, w_out)


def _attn_bias():
    si = np.arange(3 * BLOCK)[:, None]
    qi = np.arange(BLOCK)[None, :]
    dist = np.abs(qi + BLOCK - si).astype(np.float64)
    slopes = np.power(2.0, -8.0 * (np.arange(N_HEADS) + 1) / N_HEADS)
    out = np.empty((3, 2 * N_KV, 3 * BLOCK, 2 * LANES), np.float32)
    for edge in range(3):
        ok = dist <= WINDOW
        if edge == 1:
            ok = ok & (si >= BLOCK)
        if edge == 2:
            ok = ok & (si < 2 * BLOCK)
        for g in range(N_KV):
            for par in range(2):
                for half, h in enumerate((GROUP * g + par, GROUP * g + 2 + par)):
                    out[edge, 2 * g + par, :, half * LANES:(half + 1) * LANES] = np.where(
                        ok, -slopes[h] * dist * LOG2E, NEG)
    return jnp.asarray(out.reshape(3 * 2 * N_KV, 3 * BLOCK, 2 * LANES))


def _sink_table(sink):
    depth = sink.shape[0]
    s4 = sink.reshape(depth, N_KV, 2, 2) * LOG2E
    pairs = jnp.transpose(s4, (0, 1, 3, 2))
    return jnp.repeat(pairs.reshape(depth, 2 * N_KV, 1, 2), LANES, axis=-1)


def _head_mean_matrix(width):
    idx = np.arange(width) // HEAD_DIM
    return jnp.asarray((idx[:, None] == idx[None, :]).astype(np.float32) / HEAD_DIM).astype(BF16)


def _trunk(x, p, consts):
    b, s, _ = x.shape
    t = b * s
    l1, m2, cs = consts["fft"][s]
    x = x.reshape(t, D_MODEL)
    for l in range(DEPTH):
        x = _ffn(x, p["ln_ffn1"][l], p["w_ffn1_in"][l], p["w_ffn1_out"][l])
        q, kx, vt, f = _inproj(x, p["ln_mix"][l], p["w_qkvf"][l], consts["ones_q"],
                               consts["ones_k"], p["q_gain"][l], p["k_gain"][l],
                               n2=s // FFT_N1)
        ao = _attn(q.reshape(b, s, ATT_W), kx.reshape(b, s, 4 * KV_W), vt,
                   consts["bias"], p["sink"][l])
        fo = _fourier(f.reshape(b, -1, FOUR_W), l1, m2, cs)
        x = _mixout(x, ao.reshape(t, ATT_W), fo.reshape(-1, FOUR_W), p["ln_mix"][l],
                    p["w_gate_a"][l], p["w_gate_f"][l], p["w_attn_br"][l],
                    p["w_four_br"][l], p["w_out"][l])
        x = _ffn(x, p["ln_ffn2"][l], p["w_ffn2_in"][l], p["w_ffn2_out"][l])
    return x.reshape(b, s, D_MODEL)


def kernel(x_prompt, x_sample, ln_ffn1, w_ffn1_in, w_ffn1_out, ln_mix, w_in, q_gain, k_gain,
           sink, w_attn_br, w_four_br, w_out, ln_ffn2, w_ffn2_in, w_ffn2_out):
    depth = ln_ffn1.shape[0]
    w_in_b = w_in.astype(BF16)
    gate0 = QKVF_W
    p = {
        "ln_ffn1": ln_ffn1.reshape(depth, 1, D_MODEL),
        "ln_mix": ln_mix.reshape(depth, 1, D_MODEL),
        "ln_ffn2": ln_ffn2.reshape(depth, 1, D_MODEL),
        "w_ffn1_in": w_ffn1_in.astype(BF16), "w_ffn1_out": w_ffn1_out.astype(BF16),
        "w_ffn2_in": w_ffn2_in.astype(BF16), "w_ffn2_out": w_ffn2_out.astype(BF16),
        "w_qkvf": w_in_b[:, :, :gate0],
        "w_gate_a": w_in_b[:, :, gate0:gate0 + D_MODEL],
        "w_gate_f": w_in_b[:, :, gate0 + D_MODEL:],
        "q_gain": jnp.tile(q_gain, (1, N_HEADS)).reshape(depth, 1, ATT_W),
        "k_gain": jnp.tile(k_gain, (1, N_KV)).reshape(depth, 1, KV_W),
        "sink": _sink_table(sink),
        "w_attn_br": w_attn_br.astype(BF16), "w_four_br": w_four_br.astype(BF16),
        "w_out": w_out.astype(BF16),
    }
    consts = {
        "bias": _attn_bias(),
        "ones_q": _head_mean_matrix(2 * LANES),
        "ones_k": _head_mean_matrix(KV_W),
        "fft": {s: _fft_tables(s) for s in {x_prompt.shape[1], x_sample.shape[1]}},
    }
    return (_trunk(x_prompt, p, consts), _trunk(x_sample, p, consts))
```

```python
import functools

import numpy as np
import jax
import jax.numpy as jnp
from jax import lax
from jax.experimental import pallas as pl
from jax.experimental.pallas import tpu as pltpu

D_MODEL = 1024
DEPTH = 4
HEAD_DIM = 64
N_HEADS = 8
N_KV = 2
GROUP = N_HEADS // N_KV
ATT_W = N_HEADS * HEAD_DIM
KV_W = N_KV * HEAD_DIM
N_FG = 4
FG_W = 128
FOUR_W = N_FG * FG_W
QKVF_W = ATT_W + 2 * KV_W + FOUR_W
D_FF = 2816
WINDOW = 128
BLOCK = 128
EPS = 1e-6
NEG = -1e30

LANES = 128
LOG2E = 1.4426950408889634
Q_SCALE = HEAD_DIM ** -0.5 * LOG2E
VT_ONES = 16
VT_ROWS = HEAD_DIM + VT_ONES
FFT_N1 = 64
PITCH_PAD = 8
VMEM_LIMIT = 56 * 1024 * 1024

BF16 = jnp.bfloat16
F32 = jnp.float32


def _dot(a, b):
    return jnp.dot(a, b, preferred_element_type=F32)


def _rms_rows(x, gain):
    ms = jnp.mean(x * x, axis=-1, keepdims=True)
    return x * lax.rsqrt(ms + EPS) * gain


def _resident(shape):
    nd = len(shape)
    return pl.BlockSpec(shape, lambda *_: (0,) * nd, pipeline_mode=pl.Buffered(1))


def _ffn_kernel(x_ref, gain_ref, wg_ref, wu_ref, wo_ref, o_ref, *, chunk):
    x = x_ref[...]
    h = _rms_rows(x, gain_ref[...]).astype(BF16)
    acc = jnp.zeros(x.shape, F32)
    for c in range(D_FF // chunk):
        sl = slice(c * chunk, (c + 1) * chunk)
        g = _dot(h, wg_ref[:, sl])
        u = _dot(h, wu_ref[:, sl])
        a = (jax.nn.silu(g) * u).astype(BF16)
        acc = acc + _dot(a, wo_ref[sl, :])
    o_ref[...] = x + 0.5 * acc


def _ffn(x, gain, w_in, w_out, *, tm=1024, chunk=256):
    t = x.shape[0]
    return pl.pallas_call(
        functools.partial(_ffn_kernel, chunk=chunk),
        out_shape=jax.ShapeDtypeStruct(x.shape, F32),
        grid=(t // tm,),
        in_specs=[
            pl.BlockSpec((tm, D_MODEL), lambda i: (i, 0)),
            _resident((1, D_MODEL)),
            pl.BlockSpec((D_MODEL, D_FF), lambda i: (0, 0), pipeline_mode=pl.Buffered(1)),
            pl.BlockSpec((D_MODEL, D_FF), lambda i: (0, 1), pipeline_mode=pl.Buffered(1)),
            _resident((D_FF, D_MODEL)),
        ],
        out_specs=pl.BlockSpec((tm, D_MODEL), lambda i: (i, 0)),
        compiler_params=pltpu.CompilerParams(
            dimension_semantics=("arbitrary",), vmem_limit_bytes=VMEM_LIMIT),
        name="ffn",
    )(x, gain, w_in, w_in, w_out)


def _head_rms(t, ones_ref, gain):
    sq = t * t
    hi = sq.astype(BF16)
    lo = (sq - hi.astype(F32)).astype(BF16)
    ones = ones_ref[...]
    width = ones.shape[0]
    ms = jnp.concatenate(
        [_dot(hi[:, c:c + width], ones) + _dot(lo[:, c:c + width], ones)
         for c in range(0, t.shape[1], width)], axis=1)
    return t * lax.rsqrt(ms + EPS) * gain


def _inproj_kernel(x_ref, gain_ref, w_ref, onesq_ref, onesk_ref, qg_ref, kg_ref,
                   q_ref, kx_ref, vt_ref, f_ref, *, n2, sub):
    nsub = x_ref.shape[0] // sub
    zs = []
    for r in range(nsub):
        x = x_ref[r * sub:(r + 1) * sub, :]
        zs.append(_dot(_rms_rows(x, gain_ref[...]).astype(BF16), w_ref[...]))
    p2 = n2 + PITCH_PAD
    for r, z in enumerate(zs):
        rows = slice(r * sub, (r + 1) * sub)
        q = z[:, :ATT_W]
        k = z[:, ATT_W:ATT_W + KV_W]
        v = z[:, ATT_W + KV_W:ATT_W + 2 * KV_W]
        f = z[:, ATT_W + 2 * KV_W:]
        for j in range(sub // n2):
            base = (r * (sub // n2) + j) * p2
            f_ref[base:base + n2, :] = f[j * n2:(j + 1) * n2]
            f_ref[base + n2:base + p2, :] = jnp.zeros((PITCH_PAD, FOUR_W), F32)
        q_ref[rows, :] = (_head_rms(q, onesq_ref, qg_ref[...]) * Q_SCALE).astype(BF16)

        kn = _head_rms(k, onesk_ref, kg_ref[...])
        lane = lax.broadcasted_iota(jnp.int32, kn.shape, 1)
        low = lane < HEAD_DIM
        ksw = pltpu.roll(kn, HEAD_DIM, axis=1)
        zero = jnp.zeros_like(kn)
        kx = jnp.concatenate([
            jnp.where(low, kn, zero),
            jnp.where(low, zero, ksw),
            jnp.where(low, ksw, zero),
            jnp.where(low, zero, kn),
        ], axis=1)
        kx_ref[rows, :] = kx.astype(BF16)

        vt = v.T
        ones = jnp.ones((VT_ONES, sub), F32)
        vt_ref[:, rows] = jnp.concatenate(
            [vt[:HEAD_DIM], ones, vt[HEAD_DIM:], ones], axis=0).astype(BF16)


def _inproj(x, gain, w_qkvf, ones_q, ones_k, q_gain, k_gain, *, n2, tm=1024, sub=512):
    t = x.shape[0]
    assert tm % sub == 0 and sub % n2 == 0
    row = lambda w: pl.BlockSpec((tm, w), lambda i: (i, 0))
    f_rows = tm // n2 * (n2 + PITCH_PAD)
    return pl.pallas_call(
        functools.partial(_inproj_kernel, n2=n2, sub=sub),
        out_shape=(
            jax.ShapeDtypeStruct((t, ATT_W), BF16),
            jax.ShapeDtypeStruct((t, 4 * KV_W), BF16),
            jax.ShapeDtypeStruct((N_KV * VT_ROWS, t), BF16),
            jax.ShapeDtypeStruct((t // tm * f_rows, FOUR_W), F32),
        ),
        grid=(t // tm,),
        in_specs=[
            row(D_MODEL),
            _resident((1, D_MODEL)),
            _resident((D_MODEL, QKVF_W)),
            _resident((2 * LANES, 2 * LANES)),
            _resident((KV_W, KV_W)),
            _resident((1, ATT_W)),
            _resident((1, KV_W)),
        ],
        out_specs=(row(ATT_W), row(4 * KV_W),
                   pl.BlockSpec((N_KV * VT_ROWS, tm), lambda i: (0, i)),
                   pl.BlockSpec((f_rows, FOUR_W), lambda i: (i, 0))),
        compiler_params=pltpu.CompilerParams(
            dimension_semantics=("arbitrary",), vmem_limit_bytes=VMEM_LIMIT),
        name="inproj",
    )(x, gain, w_qkvf, ones_q, ones_k, q_gain, k_gain)


def _attn_kernel(sink_ref, q_ref, kx_ref, kxp_ref, kxn_ref, vt_ref, vtp_ref, vtn_ref,
                 bias_ref, o_ref, kbuf, vbuf, sbuf, *, tq, nblk):
    i = pl.program_id(1)
    nr = tq // BLOCK
    win = 3 * BLOCK
    kbuf[0:BLOCK] = kxp_ref[...]
    kbuf[BLOCK:BLOCK + tq] = kx_ref[...]
    kbuf[BLOCK + tq:] = kxn_ref[...]
    vbuf[:, 0:BLOCK] = vtp_ref[...]
    vbuf[:, BLOCK:BLOCK + tq] = vt_ref[...]
    vbuf[:, BLOCK + tq:] = vtn_ref[...]

    def scores(r, pair):
        g = pair // 2
        rows = slice(r * BLOCK, (r + 1) * BLOCK)
        qpair = jnp.concatenate(
            [q_ref[rows, (2 * g) * LANES:(2 * g + 1) * LANES],
             q_ref[rows, (2 * g + 1) * LANES:(2 * g + 2) * LANES]], axis=0)
        kw = kbuf[r * BLOCK:r * BLOCK + win, pair * LANES:(pair + 1) * LANES]
        return lax.dot_general(kw, qpair, (((1,), (1,)), ((), ())),
                               preferred_element_type=F32)

    items = [(r, pair) for r in range(nr) for pair in range(2 * N_KV)]
    nbuf = sbuf.shape[0]
    for n in range(nbuf - 1):
        sbuf[n] = scores(*items[n])
    parts = [None] * N_HEADS
    for n, (r, pair) in enumerate(items):
        ahead = n + nbuf - 1
        if ahead < len(items):
            sbuf[ahead % nbuf] = scores(*items[ahead])
        s = sbuf[n % nbuf]
        g, par = pair // 2, pair % 2
        if r == 0:
            edge = jnp.where(i == 0, 1, 0)
        elif r == nr - 1:
            edge = jnp.where(i * nr + r == nblk - 1, 2, 0)
        else:
            edge = 0
        s = s + bias_ref[edge * (2 * N_KV) + pair]
        sink = sink_ref[pair]
        m = jnp.maximum(jnp.max(s, axis=0, keepdims=True), sink)
        p = jnp.exp2(s - m).astype(BF16)
        vt = vbuf[g * VT_ROWS:(g + 1) * VT_ROWS, r * BLOCK:r * BLOCK + win]
        oe = _dot(vt, p)
        den = oe[HEAD_DIM:HEAD_DIM + 1] + jnp.exp2(sink - m)
        o = oe[:HEAD_DIM] * (1.0 / den)
        parts[GROUP * g + par] = o[:, :LANES]
        parts[GROUP * g + 2 + par] = o[:, LANES:]
        if pair == 2 * N_KV - 1:
            o_ref[r * BLOCK:(r + 1) * BLOCK, :] = jnp.concatenate(parts, axis=0).T.astype(BF16)


def _attn(q, kx, vt, bias, sink, *, tq=2048):
    b, s, _ = q.shape
    nr = tq // BLOCK
    nblk = s // BLOCK
    assert nr >= 2 and s % tq == 0
    main = lambda w: pl.BlockSpec((None, tq, w), lambda bi, i: (bi, i, 0))
    prev = lambda w: pl.BlockSpec(
        (None, BLOCK, w), lambda bi, i: (bi, jnp.maximum(i * nr - 1, 0), 0))
    nxt = lambda w: pl.BlockSpec(
        (None, BLOCK, w), lambda bi, i: (bi, jnp.minimum((i + 1) * nr, nblk - 1), 0))
    vrows = N_KV * VT_ROWS
    return pl.pallas_call(
        functools.partial(_attn_kernel, tq=tq, nblk=nblk),
        out_shape=jax.ShapeDtypeStruct((b, s, ATT_W), BF16),
        grid=(b, s // tq),
        in_specs=[
            _resident(sink.shape),
            main(ATT_W),
            main(4 * KV_W), prev(4 * KV_W), nxt(4 * KV_W),
            pl.BlockSpec((vrows, tq), lambda bi, i: (0, bi * (s // tq) + i)),
            pl.BlockSpec((vrows, BLOCK),
                         lambda bi, i: (0, bi * nblk + jnp.maximum(i * nr - 1, 0))),
            pl.BlockSpec((vrows, BLOCK),
                         lambda bi, i: (0, bi * nblk + jnp.minimum((i + 1) * nr, nblk - 1))),
            _resident(bias.shape),
        ],
        out_specs=main(ATT_W),
        scratch_shapes=[
            pltpu.VMEM((tq + 2 * BLOCK, 4 * KV_W), BF16),
            pltpu.VMEM((vrows, tq + 2 * BLOCK), BF16),
            pltpu.VMEM((6, 3 * BLOCK, 2 * LANES), F32),
        ],
        compiler_params=pltpu.CompilerParams(
            dimension_semantics=("arbitrary", "arbitrary"), vmem_limit_bytes=VMEM_LIMIT),
        name="attn",
    )(sink, q, kx, kx, kx, vt, vt, vt, bias)


def _fourier_kernel(f_ref, l1_ref, m2_ref, cs_ref, o_ref, yr_ref, yi_ref, xbuf, *, n1, n2):
    p2 = n2 + PITCH_PAD
    p1 = n1 + PITCH_PAD
    for s2 in range(n2):
        a = f_ref[pl.ds(s2, n1, stride=p2), :].astype(BF16)
        y = _dot(l1_ref[s2], a)
        yr_ref[s2 * p1:s2 * p1 + n1, :] = y[:n1]
        yi_ref[s2 * p1:s2 * p1 + n1, :] = y[n1:]

    def stage2(kp):
        ys = [jnp.concatenate([yr_ref[pl.ds(2 * kp + d, n2, stride=p1), :],
                               yi_ref[pl.ds(2 * kp + d, n2, stride=p1), :]], axis=0)
              for d in range(2)]
        return _dot(m2_ref[...], jnp.concatenate(ys, axis=1).astype(BF16))

    nbuf = xbuf.shape[0]
    npairs = n1 // 2
    for kp in range(nbuf - 1):
        xbuf[kp] = stage2(kp)
    for kp in range(npairs):
        ahead = kp + nbuf - 1
        if ahead < npairs:
            xbuf[ahead % nbuf] = stage2(ahead)
        x = xbuf[kp % nbuf]
        xc = jnp.concatenate([
            jnp.concatenate([x[:n2, :LANES], x[n2:, :LANES]], axis=1),
            jnp.concatenate([x[:n2, LANES:], x[n2:, LANES:]], axis=1)], axis=0).astype(BF16)
        o = _dot(xc, cs_ref[...])
        o_ref[pl.ds(2 * kp, n2, stride=p1), :] = o[:n2]
        o_ref[pl.ds(2 * kp + 1, n2, stride=p1), :] = o[n2:]
    for k2 in range(n2):
        o_ref[k2 * p1 + n1:(k2 + 1) * p1, :] = jnp.zeros((PITCH_PAD, FG_W), F32)


def _fourier(f, l1, m2, cs):
    b, rows_in, _ = f.shape
    n1 = FFT_N1
    n2 = rows_in // n1 - PITCH_PAD
    rows_out = n2 * (n1 + PITCH_PAD)
    return pl.pallas_call(
        functools.partial(_fourier_kernel, n1=n1, n2=n2),
        out_shape=jax.ShapeDtypeStruct((b, rows_out, FOUR_W), F32),
        grid=(b, N_FG),
        in_specs=[pl.BlockSpec((None, rows_in, FG_W), lambda bi, g: (bi, 0, g)),
                  _resident(l1.shape), _resident(m2.shape), _resident(cs.shape)],
        out_specs=pl.BlockSpec((None, rows_out, FG_W), lambda bi, g: (bi, 0, g)),
        scratch_shapes=[pltpu.VMEM((rows_out, FG_W), F32), pltpu.VMEM((rows_out, FG_W), F32),
                        pltpu.VMEM((4, 2 * n2, 2 * LANES), F32)],
        compiler_params=pltpu.CompilerParams(
            dimension_semantics=("arbitrary", "arbitrary"), vmem_limit_bytes=VMEM_LIMIT),
        name="fourier",
    )(f, l1, m2, cs)


def _fft_tables(s):
    n1, n2 = FFT_N1, s // FFT_N1
    k1 = np.arange(n1, dtype=np.int64)
    s1 = np.arange(n1, dtype=np.int64)
    s2 = np.arange(n2, dtype=np.int64)
    num = (s2[:, None, None] * k1[None, :, None]
           + n2 * s1[None, None, :] * k1[None, :, None]) % s
    th = 2.0 * np.pi * num.astype(np.float64) / s
    l1 = np.concatenate([np.cos(th), -np.sin(th)], axis=1) / np.sqrt(n1)
    k2 = np.arange(n2, dtype=np.int64)
    th2 = 2.0 * np.pi * ((k2[:, None] * s2[None, :]) % n2).astype(np.float64) / n2
    c2, sn2 = np.cos(th2), np.sin(th2)
    m2 = np.block([[c2, sn2], [-sn2, c2]]) / np.sqrt(n2)
    c = np.arange(FG_W, dtype=np.int64)
    thc = 2.0 * np.pi * ((c[:, None] * c[None, :]) % FG_W).astype(np.float64) / FG_W
    cs = np.concatenate([np.cos(thc), np.sin(thc)], axis=0) / np.sqrt(FG_W)
    as_bf16 = lambda a: jnp.asarray(a.astype(np.float32)).astype(BF16)
    return as_bf16(l1), as_bf16(m2), as_bf16(cs)


def _mixout_kernel(x_ref, ao_ref, fo_ref, gain_ref, wga_ref, wgf_ref, wa_ref, wf_ref,
                   wo_ref, o_ref, *, sub):
    p1 = FFT_N1 + PITCH_PAD
    slabs = sub // FFT_N1
    mixed = []
    for r in range(x_ref.shape[0] // sub):
        rows = slice(r * sub, (r + 1) * sub)
        h = _rms_rows(x_ref[rows, :], gain_ref[...]).astype(BF16)
        ga = jax.nn.sigmoid(_dot(h, wga_ref[...]))
        gf = jax.nn.sigmoid(_dot(h, wgf_ref[...]))
        a = _dot(ao_ref[rows, :], wa_ref[...])
        fo = jnp.concatenate([fo_ref[(r * slabs + j) * p1:(r * slabs + j) * p1 + FFT_N1, :]
                              for j in range(slabs)], axis=0)
        fb = _dot(fo.astype(BF16), wf_ref[...])
        mixed.append((ga * a + gf * fb).astype(BF16))
    for r, m in enumerate(mixed):
        rows = slice(r * sub, (r + 1) * sub)
        o_ref[rows, :] = x_ref[rows, :] + _dot(m, wo_ref[...])


def _mixout(x, ao, fo, gain, w_gate_a, w_gate_f, w_attn_br, w_four_br, w_out, *, tm=1024,
            sub=512):
    t = x.shape[0]
    row = lambda w: pl.BlockSpec((tm, w), lambda i: (i, 0))
    fo_rows = tm // FFT_N1 * (FFT_N1 + PITCH_PAD)
    return pl.pallas_call(
        functools.partial(_mixout_kernel, sub=sub),
        out_shape=jax.ShapeDtypeStruct(x.shape, F32),
        grid=(t // tm,),
        in_specs=[
            row(D_MODEL), row(ATT_W), pl.BlockSpec((fo_rows, FOUR_W), lambda i: (i, 0)),
            _resident((1, D_MODEL)),
            _resident((D_MODEL, D_MODEL)), _resident((D_MODEL, D_MODEL)),
            _resident((ATT_W, D_MODEL)), _resident((FOUR_W, D_MODEL)),
            _resident((D_MODEL, D_MODEL)),
        ],
        out_specs=row(D_MODEL),
        compiler_params=pltpu.CompilerParams(
            dimension_semantics=("arbitrary",), vmem_limit_bytes=VMEM_LIMIT),
        name="mixout",
    )(x, ao, fo, gain, w_gate_a, w_gate_f, w_attn_br, w_four_br, w_out)


def _attn_bias():
    si = np.arange(3 * BLOCK)[:, None]
    qi = np.arange(BLOCK)[None, :]
    dist = np.abs(qi + BLOCK - si).astype(np.float64)
    slopes = np.power(2.0, -8.0 * (np.arange(N_HEADS) + 1) / N_HEADS)
    out = np.empty((3, 2 * N_KV, 3 * BLOCK, 2 * LANES), np.float32)
    for edge in range(3):
        ok = dist <= WINDOW
        if edge == 1:
            ok = ok & (si >= BLOCK)
        if edge == 2:
            ok = ok & (si < 2 * BLOCK)
        for g in range(N_KV):
            for par in range(2):
                for half, h in enumerate((GROUP * g + par, GROUP * g + 2 + par)):
                    out[edge, 2 * g + par, :, half * LANES:(half + 1) * LANES] = np.where(
                        ok, -slopes[h] * dist * LOG2E, NEG)
    return jnp.asarray(out.reshape(3 * 2 * N_KV, 3 * BLOCK, 2 * LANES))


def _sink_table(sink):
    depth = sink.shape[0]
    s4 = sink.reshape(depth, N_KV, 2, 2) * LOG2E
    pairs = jnp.transpose(s4, (0, 1, 3, 2))
    return jnp.repeat(pairs.reshape(depth, 2 * N_KV, 1, 2), LANES, axis=-1)


def _head_mean_matrix(width):
    idx = np.arange(width) // HEAD_DIM
    return jnp.asarray((idx[:, None] == idx[None, :]).astype(np.float32) / HEAD_DIM).astype(BF16)


def _trunk(x, p, consts):
    b, s, _ = x.shape
    t = b * s
    l1, m2, cs = consts["fft"][s]
    x = x.reshape(t, D_MODEL)
    for l in range(DEPTH):
        x = _ffn(x, p["ln_ffn1"][l], p["w_ffn1_in"][l], p["w_ffn1_out"][l])
        q, kx, vt, f = _inproj(x, p["ln_mix"][l], p["w_qkvf"][l], consts["ones_q"],
                               consts["ones_k"], p["q_gain"][l], p["k_gain"][l],
                               n2=s // FFT_N1)
        ao = _attn(q.reshape(b, s, ATT_W), kx.reshape(b, s, 4 * KV_W), vt,
                   consts["bias"], p["sink"][l])
        fo = _fourier(f.reshape(b, -1, FOUR_W), l1, m2, cs)
        x = _mixout(x, ao.reshape(t, ATT_W), fo.reshape(-1, FOUR_W), p["ln_mix"][l],
                    p["w_gate_a"][l], p["w_gate_f"][l], p["w_attn_br"][l],
                    p["w_four_br"][l], p["w_out"][l])
        x = _ffn(x, p["ln_ffn2"][l], p["w_ffn2_in"][l], p["w_ffn2_out"][l])
    return x.reshape(b, s, D_MODEL)


def kernel(x_prompt, x_sample, ln_ffn1, w_ffn1_in, w_ffn1_out, ln_mix, w_in, q_gain, k_gain,
           sink, w_attn_br, w_four_br, w_out, ln_ffn2, w_ffn2_in, w_ffn2_out):
    depth = ln_ffn1.shape[0]
    gate0 = QKVF_W

    def per_layer(w, cols=slice(None)):
        return [w[l, :, cols].astype(BF16) for l in range(depth)]

    p = {
        "ln_ffn1": ln_ffn1.reshape(depth, 1, D_MODEL),
        "ln_mix": ln_mix.reshape(depth, 1, D_MODEL),
        "ln_ffn2": ln_ffn2.reshape(depth, 1, D_MODEL),
        "w_ffn1_in": per_layer(w_ffn1_in), "w_ffn1_out": per_layer(w_ffn1_out),
        "w_ffn2_in": per_layer(w_ffn2_in), "w_ffn2_out": per_layer(w_ffn2_out),
        "w_qkvf": per_layer(w_in, slice(0, gate0)),
        "w_gate_a": per_layer(w_in, slice(gate0, gate0 + D_MODEL)),
        "w_gate_f": per_layer(w_in, slice(gate0 + D_MODEL, None)),
        "q_gain": jnp.tile(q_gain, (1, N_HEADS)).reshape(depth, 1, ATT_W),
        "k_gain": jnp.tile(k_gain, (1, N_KV)).reshape(depth, 1, KV_W),
        "sink": _sink_table(sink),
        "w_attn_br": per_layer(w_attn_br), "w_four_br": per_layer(w_four_br),
        "w_out": per_layer(w_out),
    }
    consts = {
        "bias": _attn_bias(),
        "ones_q": _head_mean_matrix(2 * LANES),
        "ones_k": _head_mean_matrix(KV_W),
        "fft": {s: _fft_tables(s) for s in {x_prompt.shape[1], x_sample.shape[1]}},
    }
    return (_trunk(x_prompt, p, consts), _trunk(x_sample, p, consts))
```

```python
import functools

import numpy as np
import jax
import jax.numpy as jnp
from jax import lax
from jax.experimental import pallas as pl
from jax.experimental.pallas import tpu as pltpu

D_MODEL = 1024
DEPTH = 4
HEAD_DIM = 64
N_HEADS = 8
N_KV = 2
GROUP = N_HEADS // N_KV
ATT_W = N_HEADS * HEAD_DIM
KV_W = N_KV * HEAD_DIM
N_FG = 4
FG_W = 128
FOUR_W = N_FG * FG_W
QKVF_W = ATT_W + 2 * KV_W + FOUR_W
IN_COLS = QKVF_W + 2 * D_MODEL
D_FF = 2816
WINDOW = 128
BLOCK = 128
EPS = 1e-6
NEG = -1e30

LANES = 128
LOG2E = 1.4426950408889634
Q_SCALE = HEAD_DIM ** -0.5 * LOG2E
VT_ONES = 16
VT_ROWS = HEAD_DIM + VT_ONES
FFT_N1 = 64
PITCH_PAD = 8
VMEM_LIMIT = 56 * 1024 * 1024

BF16 = jnp.bfloat16
F32 = jnp.float32


def _dot(a, b):
    return jnp.dot(a, b, preferred_element_type=F32)


def _rms_rows(x, gain):
    ms = jnp.mean(x * x, axis=-1, keepdims=True)
    return x * lax.rsqrt(ms + EPS) * gain


def _resident(shape):
    nd = len(shape)
    return pl.BlockSpec(shape, lambda *_: (0,) * nd, pipeline_mode=pl.Buffered(1))


def _layer(shape, l, col=0):
    mid = (0,) * (len(shape) - 1)
    return pl.BlockSpec((None,) + tuple(shape), lambda *_: (l,) + mid + (col,),
                        pipeline_mode=pl.Buffered(1))


def _ffn_kernel(x_ref, gain_ref, wg_ref, wu_ref, wo_ref, o_ref, *, chunk):
    x = x_ref[...]
    h = _rms_rows(x, gain_ref[...]).astype(BF16)
    acc = jnp.zeros(x.shape, F32)
    for c in range(D_FF // chunk):
        sl = slice(c * chunk, (c + 1) * chunk)
        g = _dot(h, wg_ref[:, sl])
        u = _dot(h, wu_ref[:, sl])
        a = (jax.nn.silu(g) * u).astype(BF16)
        acc = acc + _dot(a, wo_ref[sl, :])
    o_ref[...] = x + 0.5 * acc


def _ffn(x, gain, w_in, w_out, l, *, tm=1024, chunk=256):
    t = x.shape[0]
    return pl.pallas_call(
        functools.partial(_ffn_kernel, chunk=chunk),
        out_shape=jax.ShapeDtypeStruct(x.shape, F32),
        grid=(t // tm,),
        in_specs=[
            pl.BlockSpec((tm, D_MODEL), lambda i: (i, 0)),
            _layer((1, D_MODEL), l),
            _layer((D_MODEL, D_FF), l, 0),
            _layer((D_MODEL, D_FF), l, 1),
            _layer((D_FF, D_MODEL), l),
        ],
        out_specs=pl.BlockSpec((tm, D_MODEL), lambda i: (i, 0)),
        compiler_params=pltpu.CompilerParams(
            dimension_semantics=("arbitrary",), vmem_limit_bytes=VMEM_LIMIT),
        name="ffn",
    )(x, gain, w_in, w_in, w_out)


def _head_rms(t, ones_ref, gain):
    sq = t * t
    hi = sq.astype(BF16)
    lo = (sq - hi.astype(F32)).astype(BF16)
    ones = ones_ref[...]
    width = ones.shape[0]
    ms = jnp.concatenate(
        [_dot(hi[:, c:c + width], ones) + _dot(lo[:, c:c + width], ones)
         for c in range(0, t.shape[1], width)], axis=1)
    return t * lax.rsqrt(ms + EPS) * gain


def _inproj_kernel(x_ref, gain_ref, w_ref, onesq_ref, onesk_ref, qg_ref, kg_ref,
                   q_ref, kx_ref, vt_ref, f_ref, *, n2, sub):
    nsub = x_ref.shape[0] // sub
    zs = []
    for r in range(nsub):
        x = x_ref[r * sub:(r + 1) * sub, :]
        zs.append(_dot(_rms_rows(x, gain_ref[...]).astype(BF16), w_ref[...]))
    p2 = n2 + PITCH_PAD
    for r, z in enumerate(zs):
        rows = slice(r * sub, (r + 1) * sub)
        q = z[:, :ATT_W]
        k = z[:, ATT_W:ATT_W + KV_W]
        v = z[:, ATT_W + KV_W:ATT_W + 2 * KV_W]
        f = z[:, ATT_W + 2 * KV_W:]
        for j in range(sub // n2):
            base = (r * (sub // n2) + j) * p2
            f_ref[base:base + n2, :] = f[j * n2:(j + 1) * n2]
            f_ref[base + n2:base + p2, :] = jnp.zeros((PITCH_PAD, FOUR_W), F32)
        q_ref[rows, :] = (_head_rms(q, onesq_ref, qg_ref[...]) * Q_SCALE).astype(BF16)

        kn = _head_rms(k, onesk_ref, kg_ref[...])
        lane = lax.broadcasted_iota(jnp.int32, kn.shape, 1)
        low = lane < HEAD_DIM
        ksw = pltpu.roll(kn, HEAD_DIM, axis=1)
        zero = jnp.zeros_like(kn)
        kx = jnp.concatenate([
            jnp.where(low, kn, zero),
            jnp.where(low, zero, ksw),
            jnp.where(low, ksw, zero),
            jnp.where(low, zero, kn),
        ], axis=1)
        kx_ref[rows, :] = kx.astype(BF16)

        vt = v.T
        ones = jnp.ones((VT_ONES, sub), F32)
        vt_ref[:, rows] = jnp.concatenate(
            [vt[:HEAD_DIM], ones, vt[HEAD_DIM:], ones], axis=0).astype(BF16)


def _inproj(x, gain, w_in, ones_q, ones_k, q_gain, k_gain, l, *, n2, tm=1024, sub=512):
    t = x.shape[0]
    assert tm % sub == 0 and sub % n2 == 0
    row = lambda w: pl.BlockSpec((tm, w), lambda i: (i, 0))
    f_rows = tm // n2 * (n2 + PITCH_PAD)
    return pl.pallas_call(
        functools.partial(_inproj_kernel, n2=n2, sub=sub),
        out_shape=(
            jax.ShapeDtypeStruct((t, ATT_W), BF16),
            jax.ShapeDtypeStruct((t, 4 * KV_W), BF16),
            jax.ShapeDtypeStruct((N_KV * VT_ROWS, t), BF16),
            jax.ShapeDtypeStruct((t // tm * f_rows, FOUR_W), F32),
        ),
        grid=(t // tm,),
        in_specs=[
            row(D_MODEL),
            _layer((1, D_MODEL), l),
            _layer((D_MODEL, QKVF_W), l),
            _resident((2 * LANES, 2 * LANES)),
            _resident((KV_W, KV_W)),
            _layer((1, ATT_W), l),
            _layer((1, KV_W), l),
        ],
        out_specs=(row(ATT_W), row(4 * KV_W),
                   pl.BlockSpec((N_KV * VT_ROWS, tm), lambda i: (0, i)),
                   pl.BlockSpec((f_rows, FOUR_W), lambda i: (i, 0))),
        compiler_params=pltpu.CompilerParams(
            dimension_semantics=("arbitrary",), vmem_limit_bytes=VMEM_LIMIT),
        name="inproj",
    )(x, gain, w_in, ones_q, ones_k, q_gain, k_gain)


def _attn_kernel(sink_ref, q_ref, kx_ref, kxp_ref, kxn_ref, vt_ref, vtp_ref, vtn_ref,
                 bias_ref, o_ref, kbuf, vbuf, sbuf, *, tq, nblk):
    i = pl.program_id(1)
    nr = tq // BLOCK
    win = 3 * BLOCK
    kbuf[0:BLOCK] = kxp_ref[...]
    kbuf[BLOCK:BLOCK + tq] = kx_ref[...]
    kbuf[BLOCK + tq:] = kxn_ref[...]
    vbuf[:, 0:BLOCK] = vtp_ref[...]
    vbuf[:, BLOCK:BLOCK + tq] = vt_ref[...]
    vbuf[:, BLOCK + tq:] = vtn_ref[...]

    def scores(r, pair):
        g = pair // 2
        rows = slice(r * BLOCK, (r + 1) * BLOCK)
        qpair = jnp.concatenate(
            [q_ref[rows, (2 * g) * LANES:(2 * g + 1) * LANES],
             q_ref[rows, (2 * g + 1) * LANES:(2 * g + 2) * LANES]], axis=0)
        kw = kbuf[r * BLOCK:r * BLOCK + win, pair * LANES:(pair + 1) * LANES]
        return lax.dot_general(kw, qpair, (((1,), (1,)), ((), ())),
                               preferred_element_type=F32)

    items = [(r, pair) for r in range(nr) for pair in range(2 * N_KV)]
    nbuf = sbuf.shape[0]
    for n in range(nbuf - 1):
        sbuf[n] = scores(*items[n])
    parts = [None] * N_HEADS
    for n, (r, pair) in enumerate(items):
        ahead = n + nbuf - 1
        if ahead < len(items):
            sbuf[ahead % nbuf] = scores(*items[ahead])
        s = sbuf[n % nbuf]
        g, par = pair // 2, pair % 2
        if r == 0:
            edge = jnp.where(i == 0, 1, 0)
        elif r == nr - 1:
            edge = jnp.where(i * nr + r == nblk - 1, 2, 0)
        else:
            edge = 0
        s = s + bias_ref[edge * (2 * N_KV) + pair]
        sink = sink_ref[pair]
        m = jnp.maximum(jnp.max(s, axis=0, keepdims=True), sink)
        p = jnp.exp2(s - m).astype(BF16)
        vt = vbuf[g * VT_ROWS:(g + 1) * VT_ROWS, r * BLOCK:r * BLOCK + win]
        oe = _dot(vt, p)
        den = oe[HEAD_DIM:HEAD_DIM + 1] + jnp.exp2(sink - m)
        o = oe[:HEAD_DIM] * (1.0 / den)
        parts[GROUP * g + par] = o[:, :LANES]
        parts[GROUP * g + 2 + par] = o[:, LANES:]
        if pair == 2 * N_KV - 1:
            o_ref[r * BLOCK:(r + 1) * BLOCK, :] = jnp.concatenate(parts, axis=0).T.astype(BF16)


def _attn(q, kx, vt, bias, sink, l, *, tq=2048):
    b, s, _ = q.shape
    nr = tq // BLOCK
    nblk = s // BLOCK
    assert nr >= 2 and s % tq == 0
    main = lambda w: pl.BlockSpec((None, tq, w), lambda bi, i: (bi, i, 0))
    prev = lambda w: pl.BlockSpec(
        (None, BLOCK, w), lambda bi, i: (bi, jnp.maximum(i * nr - 1, 0), 0))
    nxt = lambda w: pl.BlockSpec(
        (None, BLOCK, w), lambda bi, i: (bi, jnp.minimum((i + 1) * nr, nblk - 1), 0))
    vrows = N_KV * VT_ROWS
    return pl.pallas_call(
        functools.partial(_attn_kernel, tq=tq, nblk=nblk),
        out_shape=jax.ShapeDtypeStruct((b, s, ATT_W), BF16),
        grid=(b, s // tq),
        in_specs=[
            _layer(sink.shape[1:], l),
            main(ATT_W),
            main(4 * KV_W), prev(4 * KV_W), nxt(4 * KV_W),
            pl.BlockSpec((vrows, tq), lambda bi, i: (0, bi * (s // tq) + i)),
            pl.BlockSpec((vrows, BLOCK),
                         lambda bi, i: (0, bi * nblk + jnp.maximum(i * nr - 1, 0))),
            pl.BlockSpec((vrows, BLOCK),
                         lambda bi, i: (0, bi * nblk + jnp.minimum((i + 1) * nr, nblk - 1))),
            _resident(bias.shape),
        ],
        out_specs=main(ATT_W),
        scratch_shapes=[
            pltpu.VMEM((tq + 2 * BLOCK, 4 * KV_W), BF16),
            pltpu.VMEM((vrows, tq + 2 * BLOCK), BF16),
            pltpu.VMEM((6, 3 * BLOCK, 2 * LANES), F32),
        ],
        compiler_params=pltpu.CompilerParams(
            dimension_semantics=("arbitrary", "arbitrary"), vmem_limit_bytes=VMEM_LIMIT),
        name="attn",
    )(sink, q, kx, kx, kx, vt, vt, vt, bias)


def _fourier_kernel(f_ref, l1_ref, m2_ref, cs_ref, o_ref, yr_ref, yi_ref, xbuf, *, n1, n2):
    p2 = n2 + PITCH_PAD
    p1 = n1 + PITCH_PAD
    for s2 in range(n2):
        a = f_ref[pl.ds(s2, n1, stride=p2), :].astype(BF16)
        y = _dot(l1_ref[s2], a)
        yr_ref[s2 * p1:s2 * p1 + n1, :] = y[:n1]
        yi_ref[s2 * p1:s2 * p1 + n1, :] = y[n1:]

    def stage2(kp):
        ys = [jnp.concatenate([yr_ref[pl.ds(2 * kp + d, n2, stride=p1), :],
                               yi_ref[pl.ds(2 * kp + d, n2, stride=p1), :]], axis=0)
              for d in range(2)]
        return _dot(m2_ref[...], jnp.concatenate(ys, axis=1).astype(BF16))

    nbuf = xbuf.shape[0]
    npairs = n1 // 2
    for kp in range(nbuf - 1):
        xbuf[kp] = stage2(kp)
    for kp in range(npairs):
        ahead = kp + nbuf - 1
        if ahead < npairs:
            xbuf[ahead % nbuf] = stage2(ahead)
        x = xbuf[kp % nbuf]
        xc = jnp.concatenate([
            jnp.concatenate([x[:n2, :LANES], x[n2:, :LANES]], axis=1),
            jnp.concatenate([x[:n2, LANES:], x[n2:, LANES:]], axis=1)], axis=0).astype(BF16)
        o = _dot(xc, cs_ref[...])
        o_ref[pl.ds(2 * kp, n2, stride=p1), :] = o[:n2]
        o_ref[pl.ds(2 * kp + 1, n2, stride=p1), :] = o[n2:]
    for k2 in range(n2):
        o_ref[k2 * p1 + n1:(k2 + 1) * p1, :] = jnp.zeros((PITCH_PAD, FG_W), F32)


def _fourier(f, l1, m2, cs):
    b, rows_in, _ = f.shape
    n1 = FFT_N1
    n2 = rows_in // n1 - PITCH_PAD
    rows_out = n2 * (n1 + PITCH_PAD)
    return pl.pallas_call(
        functools.partial(_fourier_kernel, n1=n1, n2=n2),
        out_shape=jax.ShapeDtypeStruct((b, rows_out, FOUR_W), F32),
        grid=(b, N_FG),
        in_specs=[pl.BlockSpec((None, rows_in, FG_W), lambda bi, g: (bi, 0, g)),
                  _resident(l1.shape), _resident(m2.shape), _resident(cs.shape)],
        out_specs=pl.BlockSpec((None, rows_out, FG_W), lambda bi, g: (bi, 0, g)),
        scratch_shapes=[pltpu.VMEM((rows_out, FG_W), F32), pltpu.VMEM((rows_out, FG_W), F32),
                        pltpu.VMEM((4, 2 * n2, 2 * LANES), F32)],
        compiler_params=pltpu.CompilerParams(
            dimension_semantics=("arbitrary", "arbitrary"), vmem_limit_bytes=VMEM_LIMIT),
        name="fourier",
    )(f, l1, m2, cs)


def _fft_tables(s):
    n1, n2 = FFT_N1, s // FFT_N1
    k1 = np.arange(n1, dtype=np.int64)
    s1 = np.arange(n1, dtype=np.int64)
    s2 = np.arange(n2, dtype=np.int64)
    num = (s2[:, None, None] * k1[None, :, None]
           + n2 * s1[None, None, :] * k1[None, :, None]) % s
    th = 2.0 * np.pi * num.astype(np.float64) / s
    l1 = np.concatenate([np.cos(th), -np.sin(th)], axis=1) / np.sqrt(n1)
    k2 = np.arange(n2, dtype=np.int64)
    th2 = 2.0 * np.pi * ((k2[:, None] * s2[None, :]) % n2).astype(np.float64) / n2
    c2, sn2 = np.cos(th2), np.sin(th2)
    m2 = np.block([[c2, sn2], [-sn2, c2]]) / np.sqrt(n2)
    c = np.arange(FG_W, dtype=np.int64)
    thc = 2.0 * np.pi * ((c[:, None] * c[None, :]) % FG_W).astype(np.float64) / FG_W
    cs = np.concatenate([np.cos(thc), np.sin(thc)], axis=0) / np.sqrt(FG_W)
    as_bf16 = lambda a: jnp.asarray(a.astype(np.float32)).astype(BF16)
    return as_bf16(l1), as_bf16(m2), as_bf16(cs)


def _mixout_kernel(x_ref, ao_ref, fo_ref, gain_ref, win_ref, wa_ref, wf_ref, wo_ref, o_ref,
                   *, sub):
    p1 = FFT_N1 + PITCH_PAD
    slabs = sub // FFT_N1
    mixed = []
    for r in range(x_ref.shape[0] // sub):
        rows = slice(r * sub, (r + 1) * sub)
        h = _rms_rows(x_ref[rows, :], gain_ref[...]).astype(BF16)
        ga = jax.nn.sigmoid(_dot(h, win_ref[:, QKVF_W:QKVF_W + D_MODEL]))
        gf = jax.nn.sigmoid(_dot(h, win_ref[:, QKVF_W + D_MODEL:]))
        a = _dot(ao_ref[rows, :], wa_ref[...])
        fo = jnp.concatenate([fo_ref[(r * slabs + j) * p1:(r * slabs + j) * p1 + FFT_N1, :]
                              for j in range(slabs)], axis=0)
        fb = _dot(fo.astype(BF16), wf_ref[...])
        mixed.append((ga * a + gf * fb).astype(BF16))
    for r, m in enumerate(mixed):
        rows = slice(r * sub, (r + 1) * sub)
        o_ref[rows, :] = x_ref[rows, :] + _dot(m, wo_ref[...])


def _mixout(x, ao, fo, gain, w_in, w_attn_br, w_four_br, w_out, l, *, tm=1024, sub=512):
    t = x.shape[0]
    row = lambda w: pl.BlockSpec((tm, w), lambda i: (i, 0))
    fo_rows = tm // FFT_N1 * (FFT_N1 + PITCH_PAD)
    return pl.pallas_call(
        functools.partial(_mixout_kernel, sub=sub),
        out_shape=jax.ShapeDtypeStruct(x.shape, F32),
        grid=(t // tm,),
        in_specs=[
            row(D_MODEL), row(ATT_W), pl.BlockSpec((fo_rows, FOUR_W), lambda i: (i, 0)),
            _layer((1, D_MODEL), l),
            _layer((D_MODEL, IN_COLS), l),
            _layer((ATT_W, D_MODEL), l), _layer((FOUR_W, D_MODEL), l),
            _layer((D_MODEL, D_MODEL), l),
        ],
        out_specs=row(D_MODEL),
        compiler_params=pltpu.CompilerParams(
            dimension_semantics=("arbitrary",), vmem_limit_bytes=VMEM_LIMIT),
        name="mixout",
    )(x, ao, fo, gain, w_in, w_attn_br, w_four_br, w_out)


def _attn_bias():
    si = np.arange(3 * BLOCK)[:, None]
    qi = np.arange(BLOCK)[None, :]
    dist = np.abs(qi + BLOCK - si).astype(np.float64)
    slopes = np.power(2.0, -8.0 * (np.arange(N_HEADS) + 1) / N_HEADS)
    out = np.empty((3, 2 * N_KV, 3 * BLOCK, 2 * LANES), np.float32)
    for edge in range(3):
        ok = dist <= WINDOW
        if edge == 1:
            ok = ok & (si >= BLOCK)
        if edge == 2:
            ok = ok & (si < 2 * BLOCK)
        for g in range(N_KV):
            for par in range(2):
                for half, h in enumerate((GROUP * g + par, GROUP * g + 2 + par)):
                    out[edge, 2 * g + par, :, half * LANES:(half + 1) * LANES] = np.where(
                        ok, -slopes[h] * dist * LOG2E, NEG)
    return jnp.asarray(out.reshape(3 * 2 * N_KV, 3 * BLOCK, 2 * LANES))


def _sink_table(sink):
    depth = sink.shape[0]
    s4 = sink.reshape(depth, N_KV, 2, 2) * LOG2E
    pairs = jnp.transpose(s4, (0, 1, 3, 2))
    return jnp.repeat(pairs.reshape(depth, 2 * N_KV, 1, 2), LANES, axis=-1)


def _head_mean_matrix(width):
    idx = np.arange(width) // HEAD_DIM
    return jnp.asarray((idx[:, None] == idx[None, :]).astype(np.float32) / HEAD_DIM).astype(BF16)


def _trunk(x, p, consts):
    b, s, _ = x.shape
    t = b * s
    l1, m2, cs = consts["fft"][s]
    x = x.reshape(t, D_MODEL)
    for l in range(DEPTH):
        x = _ffn(x, p["ln_ffn1"], p["w_ffn1_in"], p["w_ffn1_out"], l)
        q, kx, vt, f = _inproj(x, p["ln_mix"], p["w_in"], consts["ones_q"], consts["ones_k"],
                               p["q_gain"], p["k_gain"], l, n2=s // FFT_N1)
        ao = _attn(q.reshape(b, s, ATT_W), kx.reshape(b, s, 4 * KV_W), vt,
                   consts["bias"], p["sink"], l)
        fo = _fourier(f.reshape(b, -1, FOUR_W), l1, m2, cs)
        x = _mixout(x, ao.reshape(t, ATT_W), fo.reshape(-1, FOUR_W), p["ln_mix"], p["w_in"],
                    p["w_attn_br"], p["w_four_br"], p["w_out"], l)
        x = _ffn(x, p["ln_ffn2"], p["w_ffn2_in"], p["w_ffn2_out"], l)
    return x.reshape(b, s, D_MODEL)


def kernel(x_prompt, x_sample, ln_ffn1, w_ffn1_in, w_ffn1_out, ln_mix, w_in, q_gain, k_gain,
           sink, w_attn_br, w_four_br, w_out, ln_ffn2, w_ffn2_in, w_ffn2_out):
    depth = ln_ffn1.shape[0]
    p = {
        "ln_ffn1": ln_ffn1.reshape(depth, 1, D_MODEL),
        "ln_mix": ln_mix.reshape(depth, 1, D_MODEL),
        "ln_ffn2": ln_ffn2.reshape(depth, 1, D_MODEL),
        "w_ffn1_in": w_ffn1_in.astype(BF16), "w_ffn1_out": w_ffn1_out.astype(BF16),
        "w_ffn2_in": w_ffn2_in.astype(BF16), "w_ffn2_out": w_ffn2_out.astype(BF16),
        "w_in": w_in.astype(BF16),
        "q_gain": jnp.tile(q_gain, (1, N_HEADS)).reshape(depth, 1, ATT_W),
        "k_gain": jnp.tile(k_gain, (1, N_KV)).reshape(depth, 1, KV_W),
        "sink": _sink_table(sink),
        "w_attn_br": w_attn_br.astype(BF16), "w_four_br": w_four_br.astype(BF16),
        "w_out": w_out.astype(BF16),
    }
    consts = {
        "bias": _attn_bias(),
        "ones_q": _head_mean_matrix(2 * LANES),
        "ones_k": _head_mean_matrix(KV_W),
        "fft": {s: _fft_tables(s) for s in {x_prompt.shape[1], x_sample.shape[1]}},
    }
    return (_trunk(x_prompt, p, consts), _trunk(x_sample, p, consts))
```

```python
import functools

import numpy as np
import jax
import jax.numpy as jnp
from jax import lax
from jax.experimental import pallas as pl
from jax.experimental.pallas import tpu as pltpu

D_MODEL = 1024
DEPTH = 4
HEAD_DIM = 64
N_HEADS = 8
N_KV = 2
GROUP = N_HEADS // N_KV
ATT_W = N_HEADS * HEAD_DIM
KV_W = N_KV * HEAD_DIM
N_FG = 4
FG_W = 128
FOUR_W = N_FG * FG_W
QKVF_W = ATT_W + 2 * KV_W + FOUR_W
IN_COLS = QKVF_W + 2 * D_MODEL
D_FF = 2816
WINDOW = 128
BLOCK = 128
EPS = 1e-6
NEG = -1e30

LANES = 128
LOG2E = 1.4426950408889634
Q_SCALE = HEAD_DIM ** -0.5 * LOG2E
VT_ONES = 16
VT_ROWS = HEAD_DIM + VT_ONES
FFT_N1 = 64
PITCH_PAD = 8
VMEM_LIMIT = 56 * 1024 * 1024

FFN_ROWS = 1024
FFN_CHUNK = 256
PROJ_ROWS = 1024
PROJ_SUB_ROWS = 512
ATTN_ROWS = 2048
ATTN_RING = 6
FOURIER_RING = 4

BF16 = jnp.bfloat16
F32 = jnp.float32


def _dot(a, b):
    return jnp.dot(a, b, preferred_element_type=F32)


def _rms_rows(x, gain):
    ms = jnp.mean(x * x, axis=-1, keepdims=True)
    return x * lax.rsqrt(ms + EPS) * gain


def _resident(shape):
    nd = len(shape)
    return pl.BlockSpec(shape, lambda *_: (0,) * nd, pipeline_mode=pl.Buffered(1))


def _layer(shape, l, col=0):
    mid = (0,) * (len(shape) - 1)
    return pl.BlockSpec((None,) + tuple(shape), lambda *_: (l,) + mid + (col,),
                        pipeline_mode=pl.Buffered(1))


def _ffn_kernel(x_ref, gain_ref, wg_ref, wu_ref, wo_ref, o_ref, *, chunk):
    x = x_ref[...]
    h = _rms_rows(x, gain_ref[...]).astype(BF16)
    acc = jnp.zeros(x.shape, F32)
    for c in range(D_FF // chunk):
        sl = slice(c * chunk, (c + 1) * chunk)
        g = _dot(h, wg_ref[:, sl])
        u = _dot(h, wu_ref[:, sl])
        a = (jax.nn.silu(g) * u).astype(BF16)
        acc = acc + _dot(a, wo_ref[sl, :])
    o_ref[...] = x + 0.5 * acc


def _ffn(x, gain, w_in, w_out, l, *, tm=FFN_ROWS, chunk=FFN_CHUNK):
    t = x.shape[0]
    return pl.pallas_call(
        functools.partial(_ffn_kernel, chunk=chunk),
        out_shape=jax.ShapeDtypeStruct(x.shape, F32),
        grid=(t // tm,),
        in_specs=[
            pl.BlockSpec((tm, D_MODEL), lambda i: (i, 0)),
            _layer((1, D_MODEL), l),
            _layer((D_MODEL, D_FF), l, 0),
            _layer((D_MODEL, D_FF), l, 1),
            _layer((D_FF, D_MODEL), l),
        ],
        out_specs=pl.BlockSpec((tm, D_MODEL), lambda i: (i, 0)),
        compiler_params=pltpu.CompilerParams(
            dimension_semantics=("arbitrary",), vmem_limit_bytes=VMEM_LIMIT),
        name="ffn",
    )(x, gain, w_in, w_in, w_out)


def _head_rms(t, ones_ref, gain):
    sq = t * t
    hi = sq.astype(BF16)
    lo = (sq - hi.astype(F32)).astype(BF16)
    ones = ones_ref[...]
    width = ones.shape[0]
    ms = jnp.concatenate(
        [_dot(hi[:, c:c + width], ones) + _dot(lo[:, c:c + width], ones)
         for c in range(0, t.shape[1], width)], axis=1)
    return t * lax.rsqrt(ms + EPS) * gain


def _inproj_kernel(x_ref, gain_ref, w_ref, onesq_ref, onesk_ref, qg_ref, kg_ref,
                   q_ref, kx_ref, vt_ref, f_ref, *, n2, sub):
    nsub = x_ref.shape[0] // sub
    zs = []
    for r in range(nsub):
        x = x_ref[r * sub:(r + 1) * sub, :]
        zs.append(_dot(_rms_rows(x, gain_ref[...]).astype(BF16), w_ref[...]))
    p2 = n2 + PITCH_PAD
    for r, z in enumerate(zs):
        rows = slice(r * sub, (r + 1) * sub)
        q = z[:, :ATT_W]
        k = z[:, ATT_W:ATT_W + KV_W]
        v = z[:, ATT_W + KV_W:ATT_W + 2 * KV_W]
        f = z[:, ATT_W + 2 * KV_W:]
        for j in range(sub // n2):
            base = (r * (sub // n2) + j) * p2
            f_ref[base:base + n2, :] = f[j * n2:(j + 1) * n2]
            f_ref[base + n2:base + p2, :] = jnp.zeros((PITCH_PAD, FOUR_W), F32)
        q_ref[rows, :] = (_head_rms(q, onesq_ref, qg_ref[...]) * Q_SCALE).astype(BF16)

        kn = _head_rms(k, onesk_ref, kg_ref[...])
        lane = lax.broadcasted_iota(jnp.int32, kn.shape, 1)
        low = lane < HEAD_DIM
        ksw = pltpu.roll(kn, HEAD_DIM, axis=1)
        zero = jnp.zeros_like(kn)
        kx = jnp.concatenate([
            jnp.where(low, kn, zero),
            jnp.where(low, zero, ksw),
            jnp.where(low, ksw, zero),
            jnp.where(low, zero, kn),
        ], axis=1)
        kx_ref[rows, :] = kx.astype(BF16)

        vt = v.T
        ones = jnp.ones((VT_ONES, sub), F32)
        vt_ref[:, rows] = jnp.concatenate(
            [vt[:HEAD_DIM], ones, vt[HEAD_DIM:], ones], axis=0).astype(BF16)


def _inproj(x, gain, w_in, ones_q, ones_k, q_gain, k_gain, l, *, n2, tm=PROJ_ROWS,
            sub=PROJ_SUB_ROWS):
    t = x.shape[0]
    assert tm % sub == 0 and sub % n2 == 0
    row = lambda w: pl.BlockSpec((tm, w), lambda i: (i, 0))
    f_rows = tm // n2 * (n2 + PITCH_PAD)
    return pl.pallas_call(
        functools.partial(_inproj_kernel, n2=n2, sub=sub),
        out_shape=(
            jax.ShapeDtypeStruct((t, ATT_W), BF16),
            jax.ShapeDtypeStruct((t, 4 * KV_W), BF16),
            jax.ShapeDtypeStruct((N_KV * VT_ROWS, t), BF16),
            jax.ShapeDtypeStruct((t // tm * f_rows, FOUR_W), F32),
        ),
        grid=(t // tm,),
        in_specs=[
            row(D_MODEL),
            _layer((1, D_MODEL), l),
            _layer((D_MODEL, QKVF_W), l),
            _resident((2 * LANES, 2 * LANES)),
            _resident((KV_W, KV_W)),
            _layer((1, ATT_W), l),
            _layer((1, KV_W), l),
        ],
        out_specs=(row(ATT_W), row(4 * KV_W),
                   pl.BlockSpec((N_KV * VT_ROWS, tm), lambda i: (0, i)),
                   pl.BlockSpec((f_rows, FOUR_W), lambda i: (i, 0))),
        compiler_params=pltpu.CompilerParams(
            dimension_semantics=("arbitrary",), vmem_limit_bytes=VMEM_LIMIT),
        name="inproj",
    )(x, gain, w_in, ones_q, ones_k, q_gain, k_gain)


def _attn_kernel(sink_ref, q_ref, kx_ref, kxp_ref, kxn_ref, vt_ref, vtp_ref, vtn_ref,
                 bias_ref, o_ref, kbuf, vbuf, sbuf, *, tq, nblk):
    i = pl.program_id(1)
    nr = tq // BLOCK
    win = 3 * BLOCK
    kbuf[0:BLOCK] = kxp_ref[...]
    kbuf[BLOCK:BLOCK + tq] = kx_ref[...]
    kbuf[BLOCK + tq:] = kxn_ref[...]
    vbuf[:, 0:BLOCK] = vtp_ref[...]
    vbuf[:, BLOCK:BLOCK + tq] = vt_ref[...]
    vbuf[:, BLOCK + tq:] = vtn_ref[...]

    def scores(r, pair):
        g = pair // 2
        rows = slice(r * BLOCK, (r + 1) * BLOCK)
        qpair = jnp.concatenate(
            [q_ref[rows, (2 * g) * LANES:(2 * g + 1) * LANES],
             q_ref[rows, (2 * g + 1) * LANES:(2 * g + 2) * LANES]], axis=0)
        kw = kbuf[r * BLOCK:r * BLOCK + win, pair * LANES:(pair + 1) * LANES]
        return lax.dot_general(kw, qpair, (((1,), (1,)), ((), ())),
                               preferred_element_type=F32)

    items = [(r, pair) for r in range(nr) for pair in range(2 * N_KV)]
    nbuf = sbuf.shape[0]
    for n in range(nbuf - 1):
        sbuf[n] = scores(*items[n])
    parts = [None] * N_HEADS
    for n, (r, pair) in enumerate(items):
        ahead = n + nbuf - 1
        if ahead < len(items):
            sbuf[ahead % nbuf] = scores(*items[ahead])
        s = sbuf[n % nbuf]
        g, par = pair // 2, pair % 2
        if r == 0:
            edge = jnp.where(i == 0, 1, 0)
        elif r == nr - 1:
            edge = jnp.where(i * nr + r == nblk - 1, 2, 0)
        else:
            edge = 0
        s = s + bias_ref[edge * (2 * N_KV) + pair]
        sink = sink_ref[pair]
        m = jnp.maximum(jnp.max(s, axis=0, keepdims=True), sink)
        p = jnp.exp2(s - m).astype(BF16)
        vt = vbuf[g * VT_ROWS:(g + 1) * VT_ROWS, r * BLOCK:r * BLOCK + win]
        oe = _dot(vt, p)
        den = oe[HEAD_DIM:HEAD_DIM + 1] + jnp.exp2(sink - m)
        o = oe[:HEAD_DIM] * (1.0 / den)
        parts[GROUP * g + par] = o[:, :LANES]
        parts[GROUP * g + 2 + par] = o[:, LANES:]
        if pair == 2 * N_KV - 1:
            o_ref[r * BLOCK:(r + 1) * BLOCK, :] = jnp.concatenate(parts, axis=0).T.astype(BF16)


def _attn(q, kx, vt, bias, sink, l, *, tq=ATTN_ROWS):
    b, s, _ = q.shape
    nr = tq // BLOCK
    nblk = s // BLOCK
    assert nr >= 2 and s % tq == 0
    main = lambda w: pl.BlockSpec((None, tq, w), lambda bi, i: (bi, i, 0))
    prev = lambda w: pl.BlockSpec(
        (None, BLOCK, w), lambda bi, i: (bi, jnp.maximum(i * nr - 1, 0), 0))
    nxt = lambda w: pl.BlockSpec(
        (None, BLOCK, w), lambda bi, i: (bi, jnp.minimum((i + 1) * nr, nblk - 1), 0))
    vrows = N_KV * VT_ROWS
    return pl.pallas_call(
        functools.partial(_attn_kernel, tq=tq, nblk=nblk),
        out_shape=jax.ShapeDtypeStruct((b, s, ATT_W), BF16),
        grid=(b, s // tq),
        in_specs=[
            _layer(sink.shape[1:], l),
            main(ATT_W),
            main(4 * KV_W), prev(4 * KV_W), nxt(4 * KV_W),
            pl.BlockSpec((vrows, tq), lambda bi, i: (0, bi * (s // tq) + i)),
            pl.BlockSpec((vrows, BLOCK),
                         lambda bi, i: (0, bi * nblk + jnp.maximum(i * nr - 1, 0))),
            pl.BlockSpec((vrows, BLOCK),
                         lambda bi, i: (0, bi * nblk + jnp.minimum((i + 1) * nr, nblk - 1))),
            _resident(bias.shape),
        ],
        out_specs=main(ATT_W),
        scratch_shapes=[
            pltpu.VMEM((tq + 2 * BLOCK, 4 * KV_W), BF16),
            pltpu.VMEM((vrows, tq + 2 * BLOCK), BF16),
            pltpu.VMEM((ATTN_RING, 3 * BLOCK, 2 * LANES), F32),
        ],
        compiler_params=pltpu.CompilerParams(
            dimension_semantics=("arbitrary", "arbitrary"), vmem_limit_bytes=VMEM_LIMIT),
        name="attn",
    )(sink, q, kx, kx, kx, vt, vt, vt, bias)


def _fourier_kernel(f_ref, l1_ref, m2_ref, cs_ref, o_ref, yr_ref, yi_ref, xbuf, *, n1, n2):
    p2 = n2 + PITCH_PAD
    p1 = n1 + PITCH_PAD
    for s2 in range(n2):
        a = f_ref[pl.ds(s2, n1, stride=p2), :].astype(BF16)
        y = _dot(l1_ref[s2], a)
        yr_ref[s2 * p1:s2 * p1 + n1, :] = y[:n1]
        yi_ref[s2 * p1:s2 * p1 + n1, :] = y[n1:]

    def stage2(kp):
        ys = [jnp.concatenate([yr_ref[pl.ds(2 * kp + d, n2, stride=p1), :],
                               yi_ref[pl.ds(2 * kp + d, n2, stride=p1), :]], axis=0)
              for d in range(2)]
        return _dot(m2_ref[...], jnp.concatenate(ys, axis=1).astype(BF16))

    nbuf = xbuf.shape[0]
    npairs = n1 // 2
    for kp in range(nbuf - 1):
        xbuf[kp] = stage2(kp)
    for kp in range(npairs):
        ahead = kp + nbuf - 1
        if ahead < npairs:
            xbuf[ahead % nbuf] = stage2(ahead)
        x = xbuf[kp % nbuf]
        xc = jnp.concatenate([
            jnp.concatenate([x[:n2, :LANES], x[n2:, :LANES]], axis=1),
            jnp.concatenate([x[:n2, LANES:], x[n2:, LANES:]], axis=1)], axis=0).astype(BF16)
        o = _dot(xc, cs_ref[...])
        o_ref[pl.ds(2 * kp, n2, stride=p1), :] = o[:n2]
        o_ref[pl.ds(2 * kp + 1, n2, stride=p1), :] = o[n2:]
    for k2 in range(n2):
        o_ref[k2 * p1 + n1:(k2 + 1) * p1, :] = jnp.zeros((PITCH_PAD, FG_W), F32)


def _fourier(f, l1, m2, cs):
    b, rows_in, _ = f.shape
    n1 = FFT_N1
    n2 = rows_in // n1 - PITCH_PAD
    rows_out = n2 * (n1 + PITCH_PAD)
    return pl.pallas_call(
        functools.partial(_fourier_kernel, n1=n1, n2=n2),
        out_shape=jax.ShapeDtypeStruct((b, rows_out, FOUR_W), F32),
        grid=(b, N_FG),
        in_specs=[pl.BlockSpec((None, rows_in, FG_W), lambda bi, g: (bi, 0, g)),
                  _resident(l1.shape), _resident(m2.shape), _resident(cs.shape)],
        out_specs=pl.BlockSpec((None, rows_out, FG_W), lambda bi, g: (bi, 0, g)),
        scratch_shapes=[pltpu.VMEM((rows_out, FG_W), F32), pltpu.VMEM((rows_out, FG_W), F32),
                        pltpu.VMEM((FOURIER_RING, 2 * n2, 2 * LANES), F32)],
        compiler_params=pltpu.CompilerParams(
            dimension_semantics=("arbitrary", "arbitrary"), vmem_limit_bytes=VMEM_LIMIT),
        name="fourier",
    )(f, l1, m2, cs)


def _fft_tables(s):
    n1, n2 = FFT_N1, s // FFT_N1
    k1 = np.arange(n1, dtype=np.int64)
    s1 = np.arange(n1, dtype=np.int64)
    s2 = np.arange(n2, dtype=np.int64)
    num = (s2[:, None, None] * k1[None, :, None]
           + n2 * s1[None, None, :] * k1[None, :, None]) % s
    th = 2.0 * np.pi * num.astype(np.float64) / s
    l1 = np.concatenate([np.cos(th), -np.sin(th)], axis=1) / np.sqrt(n1)
    k2 = np.arange(n2, dtype=np.int64)
    th2 = 2.0 * np.pi * ((k2[:, None] * s2[None, :]) % n2).astype(np.float64) / n2
    c2, sn2 = np.cos(th2), np.sin(th2)
    m2 = np.block([[c2, sn2], [-sn2, c2]]) / np.sqrt(n2)
    c = np.arange(FG_W, dtype=np.int64)
    thc = 2.0 * np.pi * ((c[:, None] * c[None, :]) % FG_W).astype(np.float64) / FG_W
    cs = np.concatenate([np.cos(thc), np.sin(thc)], axis=0) / np.sqrt(FG_W)
    as_bf16 = lambda a: jnp.asarray(a.astype(np.float32)).astype(BF16)
    return as_bf16(l1), as_bf16(m2), as_bf16(cs)


def _mixout_kernel(x_ref, ao_ref, fo_ref, gain_ref, win_ref, wa_ref, wf_ref, wo_ref, o_ref,
                   *, sub):
    p1 = FFT_N1 + PITCH_PAD
    slabs = sub // FFT_N1
    mixed = []
    for r in range(x_ref.shape[0] // sub):
        rows = slice(r * sub, (r + 1) * sub)
        h = _rms_rows(x_ref[rows, :], gain_ref[...]).astype(BF16)
        ga = jax.nn.sigmoid(_dot(h, win_ref[:, QKVF_W:QKVF_W + D_MODEL]))
        gf = jax.nn.sigmoid(_dot(h, win_ref[:, QKVF_W + D_MODEL:]))
        a = _dot(ao_ref[rows, :], wa_ref[...])
        fo = jnp.concatenate([fo_ref[(r * slabs + j) * p1:(r * slabs + j) * p1 + FFT_N1, :]
                              for j in range(slabs)], axis=0)
        fb = _dot(fo.astype(BF16), wf_ref[...])
        mixed.append((ga * a + gf * fb).astype(BF16))
    for r, m in enumerate(mixed):
        rows = slice(r * sub, (r + 1) * sub)
        o_ref[rows, :] = x_ref[rows, :] + _dot(m, wo_ref[...])


def _mixout(x, ao, fo, gain, w_in, w_attn_br, w_four_br, w_out, l, *, tm=PROJ_ROWS,
            sub=PROJ_SUB_ROWS):
    t = x.shape[0]
    row = lambda w: pl.BlockSpec((tm, w), lambda i: (i, 0))
    fo_rows = tm // FFT_N1 * (FFT_N1 + PITCH_PAD)
    return pl.pallas_call(
        functools.partial(_mixout_kernel, sub=sub),
        out_shape=jax.ShapeDtypeStruct(x.shape, F32),
        grid=(t // tm,),
        in_specs=[
            row(D_MODEL), row(ATT_W), pl.BlockSpec((fo_rows, FOUR_W), lambda i: (i, 0)),
            _layer((1, D_MODEL), l),
            _layer((D_MODEL, IN_COLS), l),
            _layer((ATT_W, D_MODEL), l), _layer((FOUR_W, D_MODEL), l),
            _layer((D_MODEL, D_MODEL), l),
        ],
        out_specs=row(D_MODEL),
        compiler_params=pltpu.CompilerParams(
            dimension_semantics=("arbitrary",), vmem_limit_bytes=VMEM_LIMIT),
        name="mixout",
    )(x, ao, fo, gain, w_in, w_attn_br, w_four_br, w_out)


def _attn_bias():
    si = np.arange(3 * BLOCK)[:, None]
    qi = np.arange(BLOCK)[None, :]
    dist = np.abs(qi + BLOCK - si).astype(np.float64)
    slopes = np.power(2.0, -8.0 * (np.arange(N_HEADS) + 1) / N_HEADS)
    out = np.empty((3, 2 * N_KV, 3 * BLOCK, 2 * LANES), np.float32)
    for edge in range(3):
        ok = dist <= WINDOW
        if edge == 1:
            ok = ok & (si >= BLOCK)
        if edge == 2:
            ok = ok & (si < 2 * BLOCK)
        for g in range(N_KV):
            for par in range(2):
                for half, h in enumerate((GROUP * g + par, GROUP * g + 2 + par)):
                    out[edge, 2 * g + par, :, half * LANES:(half + 1) * LANES] = np.where(
                        ok, -slopes[h] * dist * LOG2E, NEG)
    return jnp.asarray(out.reshape(3 * 2 * N_KV, 3 * BLOCK, 2 * LANES))


def _sink_table(sink):
    depth = sink.shape[0]
    s4 = sink.reshape(depth, N_KV, 2, 2) * LOG2E
    pairs = jnp.transpose(s4, (0, 1, 3, 2))
    return jnp.repeat(pairs.reshape(depth, 2 * N_KV, 1, 2), LANES, axis=-1)


def _head_mean_matrix(width):
    idx = np.arange(width) // HEAD_DIM
    return jnp.asarray((idx[:, None] == idx[None, :]).astype(np.float32) / HEAD_DIM).astype(BF16)


def _trunk(x, p, consts):
    b, s, _ = x.shape
    t = b * s
    l1, m2, cs = consts["fft"][s]
    x = x.reshape(t, D_MODEL)
    for l in range(DEPTH):
        x = _ffn(x, p["ln_ffn1"], p["w_ffn1_in"], p["w_ffn1_out"], l)
        q, kx, vt, f = _inproj(x, p["ln_mix"], p["w_in"], consts["ones_q"], consts["ones_k"],
                               p["q_gain"], p["k_gain"], l, n2=s // FFT_N1)
        ao = _attn(q.reshape(b, s, ATT_W), kx.reshape(b, s, 4 * KV_W), vt,
                   consts["bias"], p["sink"], l)
        fo = _fourier(f.reshape(b, -1, FOUR_W), l1, m2, cs)
        x = _mixout(x, ao.reshape(t, ATT_W), fo.reshape(-1, FOUR_W), p["ln_mix"], p["w_in"],
                    p["w_attn_br"], p["w_four_br"], p["w_out"], l)
        x = _ffn(x, p["ln_ffn2"], p["w_ffn2_in"], p["w_ffn2_out"], l)
    return x.reshape(b, s, D_MODEL)


def kernel(x_prompt, x_sample, ln_ffn1, w_ffn1_in, w_ffn1_out, ln_mix, w_in, q_gain, k_gain,
           sink, w_attn_br, w_four_br, w_out, ln_ffn2, w_ffn2_in, w_ffn2_out):
    depth = ln_ffn1.shape[0]
    p = {
        "ln_ffn1": ln_ffn1.reshape(depth, 1, D_MODEL),
        "ln_mix": ln_mix.reshape(depth, 1, D_MODEL),
        "ln_ffn2": ln_ffn2.reshape(depth, 1, D_MODEL),
        "w_ffn1_in": w_ffn1_in.astype(BF16), "w_ffn1_out": w_ffn1_out.astype(BF16),
        "w_ffn2_in": w_ffn2_in.astype(BF16), "w_ffn2_out": w_ffn2_out.astype(BF16),
        "w_in": w_in.astype(BF16),
        "q_gain": jnp.tile(q_gain, (1, N_HEADS)).reshape(depth, 1, ATT_W),
        "k_gain": jnp.tile(k_gain, (1, N_KV)).reshape(depth, 1, KV_W),
        "sink": _sink_table(sink),
        "w_attn_br": w_attn_br.astype(BF16), "w_four_br": w_four_br.astype(BF16),
        "w_out": w_out.astype(BF16),
    }
    consts = {
        "bias": _attn_bias(),
        "ones_q": _head_mean_matrix(2 * LANES),
        "ones_k": _head_mean_matrix(KV_W),
        "fft": {s: _fft_tables(s) for s in {x_prompt.shape[1], x_sample.shape[1]}},
    }
    return (_trunk(x_prompt, p, consts), _trunk(x_sample, p, consts))
```

```python
import functools

import numpy as np
import jax
import jax.numpy as jnp
from jax import lax
from jax.experimental import pallas as pl
from jax.experimental.pallas import tpu as pltpu

D_MODEL = 1024
DEPTH = 4
HEAD_DIM = 64
N_HEADS = 8
N_KV = 2
GROUP = N_HEADS // N_KV
ATT_W = N_HEADS * HEAD_DIM
KV_W = N_KV * HEAD_DIM
N_FG = 4
FG_W = 128
FOUR_W = N_FG * FG_W
QKVF_W = ATT_W + 2 * KV_W + FOUR_W
IN_COLS = QKVF_W + 2 * D_MODEL
D_FF = 2816
WINDOW = 128
BLOCK = 128
EPS = 1e-6
NEG = -1e30

LANES = 128
LOG2E = 1.4426950408889634
Q_SCALE = HEAD_DIM ** -0.5 * LOG2E
VT_ONES = 16
VT_ROWS = HEAD_DIM + VT_ONES
FFT_N1 = 64
PITCH_PAD = 8
VMEM_LIMIT = 56 * 1024 * 1024

FFN_ROWS = 1024
FFN_CHUNK = 256
PROJ_ROWS = 1024
PROJ_SUB_ROWS = 512
ATTN_ROWS = 2048
ATTN_RING = 6
FOURIER_RING = 4

BF16 = jnp.bfloat16
F32 = jnp.float32


def _dot(a, b):
    return jnp.dot(a, b, preferred_element_type=F32)


def _rms_rows(x, gain):
    ms = jnp.mean(x * x, axis=-1, keepdims=True)
    return x * lax.rsqrt(ms + EPS) * gain


def _resident(shape):
    nd = len(shape)
    return pl.BlockSpec(shape, lambda *_: (0,) * nd, pipeline_mode=pl.Buffered(1))


def _layer(shape, l, col=0):
    mid = (0,) * (len(shape) - 1)
    return pl.BlockSpec((None,) + tuple(shape), lambda *_: (l,) + mid + (col,),
                        pipeline_mode=pl.Buffered(1))


def _ffn_kernel(x_ref, gain_ref, wg_ref, wu_ref, wo_ref, o_ref, *, chunk):
    x = x_ref[...]
    h = _rms_rows(x, gain_ref[...]).astype(BF16)
    acc = jnp.zeros(x.shape, F32)
    for c in range(D_FF // chunk):
        sl = slice(c * chunk, (c + 1) * chunk)
        g = _dot(h, wg_ref[:, sl])
        u = _dot(h, wu_ref[:, sl])
        a = (jax.nn.silu(g) * u).astype(BF16)
        acc = acc + _dot(a, wo_ref[sl, :])
    o_ref[...] = x + 0.5 * acc


def _ffn(x, gain, w_in, w_out, l, *, tm=FFN_ROWS, chunk=FFN_CHUNK):
    t = x.shape[0]
    return pl.pallas_call(
        functools.partial(_ffn_kernel, chunk=chunk),
        out_shape=jax.ShapeDtypeStruct(x.shape, F32),
        grid=(t // tm,),
        in_specs=[
            pl.BlockSpec((tm, D_MODEL), lambda i: (i, 0)),
            _layer((1, D_MODEL), l),
            _layer((D_MODEL, D_FF), l, 0),
            _layer((D_MODEL, D_FF), l, 1),
            _layer((D_FF, D_MODEL), l),
        ],
        out_specs=pl.BlockSpec((tm, D_MODEL), lambda i: (i, 0)),
        compiler_params=pltpu.CompilerParams(
            dimension_semantics=("arbitrary",), vmem_limit_bytes=VMEM_LIMIT),
        name="ffn",
    )(x, gain, w_in, w_in, w_out)


def _head_rms(t, gain):
    sq = t * t
    low = lax.broadcasted_iota(jnp.int32, (t.shape[0], LANES), 1) < HEAD_DIM
    tiles = []
    for c in range(0, t.shape[1], LANES):
        pair = sq[:, c:c + LANES]
        lo = jnp.sum(jnp.where(low, pair, 0.0), axis=-1, keepdims=True)
        hi = jnp.sum(jnp.where(low, 0.0, pair), axis=-1, keepdims=True)
        tiles.append(jnp.where(low, lo, hi))
    ms = jnp.concatenate(tiles, axis=1) * (1.0 / HEAD_DIM)
    return t * lax.rsqrt(ms + EPS) * gain


def _inproj_kernel(x_ref, gain_ref, w_ref, qg_ref, kg_ref, q_ref, kx_ref, vt_ref, f_ref,
                   *, n2, sub):
    nsub = x_ref.shape[0] // sub
    zs = []
    for r in range(nsub):
        x = x_ref[r * sub:(r + 1) * sub, :]
        zs.append(_dot(_rms_rows(x, gain_ref[...]).astype(BF16), w_ref[...]))
    p2 = n2 + PITCH_PAD
    for r, z in enumerate(zs):
        rows = slice(r * sub, (r + 1) * sub)
        q = z[:, :ATT_W]
        k = z[:, ATT_W:ATT_W + KV_W]
        v = z[:, ATT_W + KV_W:ATT_W + 2 * KV_W]
        f = z[:, ATT_W + 2 * KV_W:]
        for j in range(sub // n2):
            base = (r * (sub // n2) + j) * p2
            f_ref[base:base + n2, :] = f[j * n2:(j + 1) * n2]
            f_ref[base + n2:base + p2, :] = jnp.zeros((PITCH_PAD, FOUR_W), F32)
        q_ref[rows, :] = (_head_rms(q, qg_ref[...]) * Q_SCALE).astype(BF16)

        kn = _head_rms(k, kg_ref[...])
        lane = lax.broadcasted_iota(jnp.int32, kn.shape, 1)
        low = lane < HEAD_DIM
        ksw = pltpu.roll(kn, HEAD_DIM, axis=1)
        zero = jnp.zeros_like(kn)
        kx = jnp.concatenate([
            jnp.where(low, kn, zero),
            jnp.where(low, zero, ksw),
            jnp.where(low, ksw, zero),
            jnp.where(low, zero, kn),
        ], axis=1)
        kx_ref[rows, :] = kx.astype(BF16)

        vt = v.T
        ones = jnp.ones((VT_ONES, sub), F32)
        vt_ref[:, rows] = jnp.concatenate(
            [vt[:HEAD_DIM], ones, vt[HEAD_DIM:], ones], axis=0).astype(BF16)


def _inproj(x, gain, w_in, q_gain, k_gain, l, *, n2, tm=PROJ_ROWS, sub=PROJ_SUB_ROWS):
    t = x.shape[0]
    assert tm % sub == 0 and sub % n2 == 0
    row = lambda w: pl.BlockSpec((tm, w), lambda i: (i, 0))
    f_rows = tm // n2 * (n2 + PITCH_PAD)
    return pl.pallas_call(
        functools.partial(_inproj_kernel, n2=n2, sub=sub),
        out_shape=(
            jax.ShapeDtypeStruct((t, ATT_W), BF16),
            jax.ShapeDtypeStruct((t, 4 * KV_W), BF16),
            jax.ShapeDtypeStruct((N_KV * VT_ROWS, t), BF16),
            jax.ShapeDtypeStruct((t // tm * f_rows, FOUR_W), F32),
        ),
        grid=(t // tm,),
        in_specs=[
            row(D_MODEL),
            _layer((1, D_MODEL), l),
            _layer((D_MODEL, QKVF_W), l),
            _layer((1, ATT_W), l),
            _layer((1, KV_W), l),
        ],
        out_specs=(row(ATT_W), row(4 * KV_W),
                   pl.BlockSpec((N_KV * VT_ROWS, tm), lambda i: (0, i)),
                   pl.BlockSpec((f_rows, FOUR_W), lambda i: (i, 0))),
        compiler_params=pltpu.CompilerParams(
            dimension_semantics=("arbitrary",), vmem_limit_bytes=VMEM_LIMIT),
        name="inproj",
    )(x, gain, w_in, q_gain, k_gain)


def _attn_kernel(sink_ref, q_ref, kx_ref, kxp_ref, kxn_ref, vt_ref, vtp_ref, vtn_ref,
                 bias_ref, o_ref, kbuf, vbuf, sbuf, *, tq, nblk):
    i = pl.program_id(1)
    nr = tq // BLOCK
    win = 3 * BLOCK
    kbuf[0:BLOCK] = kxp_ref[...]
    kbuf[BLOCK:BLOCK + tq] = kx_ref[...]
    kbuf[BLOCK + tq:] = kxn_ref[...]
    vbuf[:, 0:BLOCK] = vtp_ref[...]
    vbuf[:, BLOCK:BLOCK + tq] = vt_ref[...]
    vbuf[:, BLOCK + tq:] = vtn_ref[...]

    def scores(r, pair):
        g = pair // 2
        rows = slice(r * BLOCK, (r + 1) * BLOCK)
        qpair = jnp.concatenate(
            [q_ref[rows, (2 * g) * LANES:(2 * g + 1) * LANES],
             q_ref[rows, (2 * g + 1) * LANES:(2 * g + 2) * LANES]], axis=0)
        kw = kbuf[r * BLOCK:r * BLOCK + win, pair * LANES:(pair + 1) * LANES]
        return lax.dot_general(kw, qpair, (((1,), (1,)), ((), ())),
                               preferred_element_type=F32)

    items = [(r, pair) for r in range(nr) for pair in range(2 * N_KV)]
    nbuf = sbuf.shape[0]
    for n in range(nbuf - 1):
        sbuf[n] = scores(*items[n])
    parts = [None] * N_HEADS
    for n, (r, pair) in enumerate(items):
        ahead = n + nbuf - 1
        if ahead < len(items):
            sbuf[ahead % nbuf] = scores(*items[ahead])
        s = sbuf[n % nbuf]
        g, par = pair // 2, pair % 2
        if r == 0:
            edge = jnp.where(i == 0, 1, 0)
        elif r == nr - 1:
            edge = jnp.where(i * nr + r == nblk - 1, 2, 0)
        else:
            edge = 0
        s = s + bias_ref[edge * (2 * N_KV) + pair]
        sink = sink_ref[pair]
        m = jnp.maximum(jnp.max(s, axis=0, keepdims=True), sink)
        p = jnp.exp2(s - m).astype(BF16)
        vt = vbuf[g * VT_ROWS:(g + 1) * VT_ROWS, r * BLOCK:r * BLOCK + win]
        oe = _dot(vt, p)
        den = oe[HEAD_DIM:HEAD_DIM + 1] + jnp.exp2(sink - m)
        o = oe[:HEAD_DIM] * (1.0 / den)
        parts[GROUP * g + par] = o[:, :LANES]
        parts[GROUP * g + 2 + par] = o[:, LANES:]
        if pair == 2 * N_KV - 1:
            o_ref[r * BLOCK:(r + 1) * BLOCK, :] = jnp.concatenate(parts, axis=0).T.astype(BF16)


def _attn(q, kx, vt, bias, sink, l, *, tq=ATTN_ROWS):
    b, s, _ = q.shape
    nr = tq // BLOCK
    nblk = s // BLOCK
    assert nr >= 2 and s % tq == 0
    main = lambda w: pl.BlockSpec((None, tq, w), lambda bi, i: (bi, i, 0))
    prev = lambda w: pl.BlockSpec(
        (None, BLOCK, w), lambda bi, i: (bi, jnp.maximum(i * nr - 1, 0), 0))
    nxt = lambda w: pl.BlockSpec(
        (None, BLOCK, w), lambda bi, i: (bi, jnp.minimum((i + 1) * nr, nblk - 1), 0))
    vrows = N_KV * VT_ROWS
    return pl.pallas_call(
        functools.partial(_attn_kernel, tq=tq, nblk=nblk),
        out_shape=jax.ShapeDtypeStruct((b, s, ATT_W), BF16),
        grid=(b, s // tq),
        in_specs=[
            _layer(sink.shape[1:], l),
            main(ATT_W),
            main(4 * KV_W), prev(4 * KV_W), nxt(4 * KV_W),
            pl.BlockSpec((vrows, tq), lambda bi, i: (0, bi * (s // tq) + i)),
            pl.BlockSpec((vrows, BLOCK),
                         lambda bi, i: (0, bi * nblk + jnp.maximum(i * nr - 1, 0))),
            pl.BlockSpec((vrows, BLOCK),
                         lambda bi, i: (0, bi * nblk + jnp.minimum((i + 1) * nr, nblk - 1))),
            _resident(bias.shape),
        ],
        out_specs=main(ATT_W),
        scratch_shapes=[
            pltpu.VMEM((tq + 2 * BLOCK, 4 * KV_W), BF16),
            pltpu.VMEM((vrows, tq + 2 * BLOCK), BF16),
            pltpu.VMEM((ATTN_RING, 3 * BLOCK, 2 * LANES), F32),
        ],
        compiler_params=pltpu.CompilerParams(
            dimension_semantics=("arbitrary", "arbitrary"), vmem_limit_bytes=VMEM_LIMIT),
        name="attn",
    )(sink, q, kx, kx, kx, vt, vt, vt, bias)


def _fourier_kernel(f_ref, l1_ref, m2_ref, cs_ref, o_ref, yr_ref, yi_ref, xbuf, *, n1, n2):
    p2 = n2 + PITCH_PAD
    p1 = n1 + PITCH_PAD
    for s2 in range(n2):
        a = f_ref[pl.ds(s2, n1, stride=p2), :].astype(BF16)
        y = _dot(l1_ref[s2], a)
        yr_ref[s2 * p1:s2 * p1 + n1, :] = y[:n1]
        yi_ref[s2 * p1:s2 * p1 + n1, :] = y[n1:]

    def stage2(kp):
        ys = [jnp.concatenate([yr_ref[pl.ds(2 * kp + d, n2, stride=p1), :],
                               yi_ref[pl.ds(2 * kp + d, n2, stride=p1), :]], axis=0)
              for d in range(2)]
        return _dot(m2_ref[...], jnp.concatenate(ys, axis=1).astype(BF16))

    nbuf = xbuf.shape[0]
    npairs = n1 // 2
    for kp in range(nbuf - 1):
        xbuf[kp] = stage2(kp)
    for kp in range(npairs):
        ahead = kp + nbuf - 1
        if ahead < npairs:
            xbuf[ahead % nbuf] = stage2(ahead)
        x = xbuf[kp % nbuf]
        xc = jnp.concatenate([
            jnp.concatenate([x[:n2, :LANES], x[n2:, :LANES]], axis=1),
            jnp.concatenate([x[:n2, LANES:], x[n2:, LANES:]], axis=1)], axis=0).astype(BF16)
        o = _dot(xc, cs_ref[...])
        o_ref[pl.ds(2 * kp, n2, stride=p1), :] = o[:n2]
        o_ref[pl.ds(2 * kp + 1, n2, stride=p1), :] = o[n2:]
    for k2 in range(n2):
        o_ref[k2 * p1 + n1:(k2 + 1) * p1, :] = jnp.zeros((PITCH_PAD, FG_W), F32)


def _fourier(f, l1, m2, cs):
    b, rows_in, _ = f.shape
    n1 = FFT_N1
    n2 = rows_in // n1 - PITCH_PAD
    rows_out = n2 * (n1 + PITCH_PAD)
    return pl.pallas_call(
        functools.partial(_fourier_kernel, n1=n1, n2=n2),
        out_shape=jax.ShapeDtypeStruct((b, rows_out, FOUR_W), F32),
        grid=(b, N_FG),
        in_specs=[pl.BlockSpec((None, rows_in, FG_W), lambda bi, g: (bi, 0, g)),
                  _resident(l1.shape), _resident(m2.shape), _resident(cs.shape)],
        out_specs=pl.BlockSpec((None, rows_out, FG_W), lambda bi, g: (bi, 0, g)),
        scratch_shapes=[pltpu.VMEM((rows_out, FG_W), F32), pltpu.VMEM((rows_out, FG_W), F32),
                        pltpu.VMEM((FOURIER_RING, 2 * n2, 2 * LANES), F32)],
        compiler_params=pltpu.CompilerParams(
            dimension_semantics=("arbitrary", "arbitrary"), vmem_limit_bytes=VMEM_LIMIT),
        name="fourier",
    )(f, l1, m2, cs)


def _fft_tables(s):
    n1, n2 = FFT_N1, s // FFT_N1
    k1 = np.arange(n1, dtype=np.int64)
    s1 = np.arange(n1, dtype=np.int64)
    s2 = np.arange(n2, dtype=np.int64)
    num = (s2[:, None, None] * k1[None, :, None]
           + n2 * s1[None, None, :] * k1[None, :, None]) % s
    th = 2.0 * np.pi * num.astype(np.float64) / s
    l1 = np.concatenate([np.cos(th), -np.sin(th)], axis=1) / np.sqrt(n1)
    k2 = np.arange(n2, dtype=np.int64)
    th2 = 2.0 * np.pi * ((k2[:, None] * s2[None, :]) % n2).astype(np.float64) / n2
    c2, sn2 = np.cos(th2), np.sin(th2)
    m2 = np.block([[c2, sn2], [-sn2, c2]]) / np.sqrt(n2)
    c = np.arange(FG_W, dtype=np.int64)
    thc = 2.0 * np.pi * ((c[:, None] * c[None, :]) % FG_W).astype(np.float64) / FG_W
    cs = np.concatenate([np.cos(thc), np.sin(thc)], axis=0) / np.sqrt(FG_W)
    as_bf16 = lambda a: jnp.asarray(a.astype(np.float32)).astype(BF16)
    return as_bf16(l1), as_bf16(m2), as_bf16(cs)


def _mixout_kernel(x_ref, ao_ref, fo_ref, gain_ref, win_ref, wa_ref, wf_ref, wo_ref, o_ref,
                   *, sub):
    p1 = FFT_N1 + PITCH_PAD
    slabs = sub // FFT_N1
    mixed = []
    for r in range(x_ref.shape[0] // sub):
        rows = slice(r * sub, (r + 1) * sub)
        h = _rms_rows(x_ref[rows, :], gain_ref[...]).astype(BF16)
        ga = jax.nn.sigmoid(_dot(h, win_ref[:, QKVF_W:QKVF_W + D_MODEL]))
        gf = jax.nn.sigmoid(_dot(h, win_ref[:, QKVF_W + D_MODEL:]))
        a = _dot(ao_ref[rows, :], wa_ref[...])
        fo = jnp.concatenate([fo_ref[(r * slabs + j) * p1:(r * slabs + j) * p1 + FFT_N1, :]
                              for j in range(slabs)], axis=0)
        fb = _dot(fo.astype(BF16), wf_ref[...])
        mixed.append((ga * a + gf * fb).astype(BF16))
    for r, m in enumerate(mixed):
        rows = slice(r * sub, (r + 1) * sub)
        o_ref[rows, :] = x_ref[rows, :] + _dot(m, wo_ref[...])


def _mixout(x, ao, fo, gain, w_in, w_attn_br, w_four_br, w_out, l, *, tm=PROJ_ROWS,
            sub=PROJ_SUB_ROWS):
    t = x.shape[0]
    row = lambda w: pl.BlockSpec((tm, w), lambda i: (i, 0))
    fo_rows = tm // FFT_N1 * (FFT_N1 + PITCH_PAD)
    return pl.pallas_call(
        functools.partial(_mixout_kernel, sub=sub),
        out_shape=jax.ShapeDtypeStruct(x.shape, F32),
        grid=(t // tm,),
        in_specs=[
            row(D_MODEL), row(ATT_W), pl.BlockSpec((fo_rows, FOUR_W), lambda i: (i, 0)),
            _layer((1, D_MODEL), l),
            _layer((D_MODEL, IN_COLS), l),
            _layer((ATT_W, D_MODEL), l), _layer((FOUR_W, D_MODEL), l),
            _layer((D_MODEL, D_MODEL), l),
        ],
        out_specs=row(D_MODEL),
        compiler_params=pltpu.CompilerParams(
            dimension_semantics=("arbitrary",), vmem_limit_bytes=VMEM_LIMIT),
        name="mixout",
    )(x, ao, fo, gain, w_in, w_attn_br, w_four_br, w_out)


def _attn_bias():
    si = np.arange(3 * BLOCK)[:, None]
    qi = np.arange(BLOCK)[None, :]
    dist = np.abs(qi + BLOCK - si).astype(np.float64)
    slopes = np.power(2.0, -8.0 * (np.arange(N_HEADS) + 1) / N_HEADS)
    out = np.empty((3, 2 * N_KV, 3 * BLOCK, 2 * LANES), np.float32)
    for edge in range(3):
        ok = dist <= WINDOW
        if edge == 1:
            ok = ok & (si >= BLOCK)
        if edge == 2:
            ok = ok & (si < 2 * BLOCK)
        for g in range(N_KV):
            for par in range(2):
                for half, h in enumerate((GROUP * g + par, GROUP * g + 2 + par)):
                    out[edge, 2 * g + par, :, half * LANES:(half + 1) * LANES] = np.where(
                        ok, -slopes[h] * dist * LOG2E, NEG)
    return jnp.asarray(out.reshape(3 * 2 * N_KV, 3 * BLOCK, 2 * LANES))


def _sink_table(sink):
    depth = sink.shape[0]
    s4 = sink.reshape(depth, N_KV, 2, 2) * LOG2E
    pairs = jnp.transpose(s4, (0, 1, 3, 2))
    return jnp.repeat(pairs.reshape(depth, 2 * N_KV, 1, 2), LANES, axis=-1)


def _trunk(x, p, consts):
    b, s, _ = x.shape
    t = b * s
    l1, m2, cs = consts["fft"][s]
    x = x.reshape(t, D_MODEL)
    for l in range(DEPTH):
        x = _ffn(x, p["ln_ffn1"], p["w_ffn1_in"], p["w_ffn1_out"], l)
        q, kx, vt, f = _inproj(x, p["ln_mix"], p["w_in"], p["q_gain"], p["k_gain"], l,
                               n2=s // FFT_N1)
        ao = _attn(q.reshape(b, s, ATT_W), kx.reshape(b, s, 4 * KV_W), vt,
                   consts["bias"], p["sink"], l)
        fo = _fourier(f.reshape(b, -1, FOUR_W), l1, m2, cs)
        x = _mixout(x, ao.reshape(t, ATT_W), fo.reshape(-1, FOUR_W), p["ln_mix"], p["w_in"],
                    p["w_attn_br"], p["w_four_br"], p["w_out"], l)
        x = _ffn(x, p["ln_ffn2"], p["w_ffn2_in"], p["w_ffn2_out"], l)
    return x.reshape(b, s, D_MODEL)


def kernel(x_prompt, x_sample, ln_ffn1, w_ffn1_in, w_ffn1_out, ln_mix, w_in, q_gain, k_gain,
           sink, w_attn_br, w_four_br, w_out, ln_ffn2, w_ffn2_in, w_ffn2_out):
    depth = ln_ffn1.shape[0]
    p = {
        "ln_ffn1": ln_ffn1.reshape(depth, 1, D_MODEL),
        "ln_mix": ln_mix.reshape(depth, 1, D_MODEL),
        "ln_ffn2": ln_ffn2.reshape(depth, 1, D_MODEL),
        "w_ffn1_in": w_ffn1_in.astype(BF16), "w_ffn1_out": w_ffn1_out.astype(BF16),
        "w_ffn2_in": w_ffn2_in.astype(BF16), "w_ffn2_out": w_ffn2_out.astype(BF16),
        "w_in": w_in.astype(BF16),
        "q_gain": jnp.tile(q_gain, (1, N_HEADS)).reshape(depth, 1, ATT_W),
        "k_gain": jnp.tile(k_gain, (1, N_KV)).reshape(depth, 1, KV_W),
        "sink": _sink_table(sink),
        "w_attn_br": w_attn_br.astype(BF16), "w_four_br": w_four_br.astype(BF16),
        "w_out": w_out.astype(BF16),
    }
    consts = {
        "bias": _attn_bias(),
        "fft": {s: _fft_tables(s) for s in {x_prompt.shape[1], x_sample.shape[1]}},
    }
    return (_trunk(x_prompt, p, consts), _trunk(x_sample, p, consts))
```

```python
import functools

import numpy as np
import jax
import jax.numpy as jnp
from jax import lax
from jax.experimental import pallas as pl
from jax.experimental.pallas import tpu as pltpu

D_MODEL = 1024
DEPTH = 4
HEAD_DIM = 64
N_HEADS = 8
N_KV = 2
GROUP = N_HEADS // N_KV
ATT_W = N_HEADS * HEAD_DIM
KV_W = N_KV * HEAD_DIM
N_FG = 4
FG_W = 128
FOUR_W = N_FG * FG_W
QKVF_W = ATT_W + 2 * KV_W + FOUR_W
IN_COLS = QKVF_W + 2 * D_MODEL
D_FF = 2816
WINDOW = 128
BLOCK = 128
EPS = 1e-6
NEG = -1e30

LANES = 128
LOG2E = 1.4426950408889634
Q_SCALE = HEAD_DIM ** -0.5 * LOG2E
VT_ONES = 16
VT_ROWS = HEAD_DIM + VT_ONES
FFT_N1 = 64
PITCH_PAD = 8
VMEM_LIMIT = 56 * 1024 * 1024

FFN_ROWS = 1024
FFN_CHUNK = 256
PROJ_ROWS = 1024
PROJ_SUB_ROWS = 512
ATTN_ROWS = 4096
ATTN_RING = 6
FOURIER_RING = 4

BF16 = jnp.bfloat16
F32 = jnp.float32


def _dot(a, b):
    return jnp.dot(a, b, preferred_element_type=F32)


def _rms_rows(x, gain):
    ms = jnp.mean(x * x, axis=-1, keepdims=True)
    return x * lax.rsqrt(ms + EPS) * gain


def _resident(shape):
    nd = len(shape)
    return pl.BlockSpec(shape, lambda *_: (0,) * nd, pipeline_mode=pl.Buffered(1))


def _layer(shape, l, col=0):
    mid = (0,) * (len(shape) - 1)
    return pl.BlockSpec((None,) + tuple(shape), lambda *_: (l,) + mid + (col,),
                        pipeline_mode=pl.Buffered(1))


def _ffn_kernel(x_ref, gain_ref, wg_ref, wu_ref, wo_ref, o_ref, *, chunk):
    x = x_ref[...]
    h = _rms_rows(x, gain_ref[...]).astype(BF16)
    acc = jnp.zeros(x.shape, F32)
    for c in range(D_FF // chunk):
        sl = slice(c * chunk, (c + 1) * chunk)
        g = _dot(h, wg_ref[:, sl])
        u = _dot(h, wu_ref[:, sl])
        a = (jax.nn.silu(g) * u).astype(BF16)
        acc = acc + _dot(a, wo_ref[sl, :])
    o_ref[...] = x + 0.5 * acc


def _ffn(x, gain, w_in, w_out, l, *, tm=FFN_ROWS, chunk=FFN_CHUNK):
    t = x.shape[0]
    return pl.pallas_call(
        functools.partial(_ffn_kernel, chunk=chunk),
        out_shape=jax.ShapeDtypeStruct(x.shape, F32),
        grid=(t // tm,),
        in_specs=[
            pl.BlockSpec((tm, D_MODEL), lambda i: (i, 0)),
            _layer((1, D_MODEL), l),
            _layer((D_MODEL, D_FF), l, 0),
            _layer((D_MODEL, D_FF), l, 1),
            _layer((D_FF, D_MODEL), l),
        ],
        out_specs=pl.BlockSpec((tm, D_MODEL), lambda i: (i, 0)),
        compiler_params=pltpu.CompilerParams(
            dimension_semantics=("arbitrary",), vmem_limit_bytes=VMEM_LIMIT),
        name="ffn",
    )(x, gain, w_in, w_in, w_out)


def _head_rms(t, gain):
    sq = t * t
    low = lax.broadcasted_iota(jnp.int32, (t.shape[0], LANES), 1) < HEAD_DIM
    tiles = []
    for c in range(0, t.shape[1], LANES):
        pair = sq[:, c:c + LANES]
        lo = jnp.sum(jnp.where(low, pair, 0.0), axis=-1, keepdims=True)
        hi = jnp.sum(jnp.where(low, 0.0, pair), axis=-1, keepdims=True)
        tiles.append(jnp.where(low, lo, hi))
    ms = jnp.concatenate(tiles, axis=1) * (1.0 / HEAD_DIM)
    return t * lax.rsqrt(ms + EPS) * gain


def _inproj_kernel(x_ref, gain_ref, w_ref, qg_ref, kg_ref, q_ref, kx_ref, vt_ref, f_ref,
                   *, n2, sub):
    nsub = x_ref.shape[0] // sub
    zs = []
    for r in range(nsub):
        x = x_ref[r * sub:(r + 1) * sub, :]
        zs.append(_dot(_rms_rows(x, gain_ref[...]).astype(BF16), w_ref[...]))
    p2 = n2 + PITCH_PAD
    for r, z in enumerate(zs):
        rows = slice(r * sub, (r + 1) * sub)
        q = z[:, :ATT_W]
        k = z[:, ATT_W:ATT_W + KV_W]
        v = z[:, ATT_W + KV_W:ATT_W + 2 * KV_W]
        f = z[:, ATT_W + 2 * KV_W:]
        for j in range(sub // n2):
            base = (r * (sub // n2) + j) * p2
            f_ref[base:base + n2, :] = f[j * n2:(j + 1) * n2]
            f_ref[base + n2:base + p2, :] = jnp.zeros((PITCH_PAD, FOUR_W), F32)
        q_ref[rows, :] = (_head_rms(q, qg_ref[...]) * Q_SCALE).astype(BF16)

        kn = _head_rms(k, kg_ref[...])
        lane = lax.broadcasted_iota(jnp.int32, kn.shape, 1)
        low = lane < HEAD_DIM
        ksw = pltpu.roll(kn, HEAD_DIM, axis=1)
        zero = jnp.zeros_like(kn)
        kx = jnp.concatenate([
            jnp.where(low, kn, zero),
            jnp.where(low, zero, ksw),
            jnp.where(low, ksw, zero),
            jnp.where(low, zero, kn),
        ], axis=1)
        kx_ref[rows, :] = kx.astype(BF16)

        vt = v.T
        ones = jnp.ones((VT_ONES, sub), F32)
        vt_ref[:, rows] = jnp.concatenate(
            [vt[:HEAD_DIM], ones, vt[HEAD_DIM:], ones], axis=0).astype(BF16)


def _inproj(x, gain, w_in, q_gain, k_gain, l, *, n2, tm=PROJ_ROWS, sub=PROJ_SUB_ROWS):
    t = x.shape[0]
    assert tm % sub == 0 and sub % n2 == 0
    row = lambda w: pl.BlockSpec((tm, w), lambda i: (i, 0))
    f_rows = tm // n2 * (n2 + PITCH_PAD)
    return pl.pallas_call(
        functools.partial(_inproj_kernel, n2=n2, sub=sub),
        out_shape=(
            jax.ShapeDtypeStruct((t, ATT_W), BF16),
            jax.ShapeDtypeStruct((t, 4 * KV_W), BF16),
            jax.ShapeDtypeStruct((N_KV * VT_ROWS, t), BF16),
            jax.ShapeDtypeStruct((t // tm * f_rows, FOUR_W), F32),
        ),
        grid=(t // tm,),
        in_specs=[
            row(D_MODEL),
            _layer((1, D_MODEL), l),
            _layer((D_MODEL, QKVF_W), l),
            _layer((1, ATT_W), l),
            _layer((1, KV_W), l),
        ],
        out_specs=(row(ATT_W), row(4 * KV_W),
                   pl.BlockSpec((N_KV * VT_ROWS, tm), lambda i: (0, i)),
                   pl.BlockSpec((f_rows, FOUR_W), lambda i: (i, 0))),
        compiler_params=pltpu.CompilerParams(
            dimension_semantics=("arbitrary",), vmem_limit_bytes=VMEM_LIMIT),
        name="inproj",
    )(x, gain, w_in, q_gain, k_gain)


def _attn_kernel(sink_ref, q_ref, kx_ref, kxp_ref, kxn_ref, vt_ref, vtp_ref, vtn_ref,
                 bias_ref, o_ref, sbuf, *, tq, nblk):
    i = pl.program_id(1)
    nr = tq // BLOCK

    def window(r, main, prev, nxt, axis, other):
        def cut(ref, lo, hi):
            return ref[lo:hi, other] if axis == 0 else ref[other, lo:hi]
        if r == 0:
            parts = [cut(prev, 0, BLOCK), cut(main, 0, 2 * BLOCK)]
        elif r == nr - 1:
            parts = [cut(main, (r - 1) * BLOCK, (r + 1) * BLOCK), cut(nxt, 0, BLOCK)]
        else:
            return cut(main, (r - 1) * BLOCK, (r + 2) * BLOCK)
        return jnp.concatenate(parts, axis=axis)

    def scores(r, pair):
        g = pair // 2
        rows = slice(r * BLOCK, (r + 1) * BLOCK)
        qpair = jnp.concatenate(
            [q_ref[rows, (2 * g) * LANES:(2 * g + 1) * LANES],
             q_ref[rows, (2 * g + 1) * LANES:(2 * g + 2) * LANES]], axis=0)
        kw = window(r, kx_ref, kxp_ref, kxn_ref, 0, slice(pair * LANES, (pair + 1) * LANES))
        return lax.dot_general(kw, qpair, (((1,), (1,)), ((), ())),
                               preferred_element_type=F32)

    items = [(r, pair) for r in range(nr) for pair in range(2 * N_KV)]
    nbuf = sbuf.shape[0]
    for n in range(nbuf - 1):
        sbuf[n] = scores(*items[n])
    parts = [None] * N_HEADS
    for n, (r, pair) in enumerate(items):
        ahead = n + nbuf - 1
        if ahead < len(items):
            sbuf[ahead % nbuf] = scores(*items[ahead])
        s = sbuf[n % nbuf]
        g, par = pair // 2, pair % 2
        if r == 0:
            edge = jnp.where(i == 0, 1, 0)
        elif r == nr - 1:
            edge = jnp.where(i * nr + r == nblk - 1, 2, 0)
        else:
            edge = 0
        s = s + bias_ref[edge * (2 * N_KV) + pair]
        sink = sink_ref[pair]
        m = jnp.maximum(jnp.max(s, axis=0, keepdims=True), sink)
        p = jnp.exp2(s - m).astype(BF16)
        vt = window(r, vt_ref, vtp_ref, vtn_ref, 1, slice(g * VT_ROWS, (g + 1) * VT_ROWS))
        oe = _dot(vt, p)
        den = oe[HEAD_DIM:HEAD_DIM + 1] + jnp.exp2(sink - m)
        o = oe[:HEAD_DIM] * (1.0 / den)
        parts[GROUP * g + par] = o[:, :LANES]
        parts[GROUP * g + 2 + par] = o[:, LANES:]
        if pair == 2 * N_KV - 1:
            o_ref[r * BLOCK:(r + 1) * BLOCK, :] = jnp.concatenate(parts, axis=0).T.astype(BF16)


def _attn(q, kx, vt, bias, sink, l, *, tq=ATTN_ROWS):
    b, s, _ = q.shape
    nr = tq // BLOCK
    nblk = s // BLOCK
    assert nr >= 2 and s % tq == 0
    main = lambda w: pl.BlockSpec((None, tq, w), lambda bi, i: (bi, i, 0))
    prev = lambda w: pl.BlockSpec(
        (None, BLOCK, w), lambda bi, i: (bi, jnp.maximum(i * nr - 1, 0), 0))
    nxt = lambda w: pl.BlockSpec(
        (None, BLOCK, w), lambda bi, i: (bi, jnp.minimum((i + 1) * nr, nblk - 1), 0))
    vrows = N_KV * VT_ROWS
    return pl.pallas_call(
        functools.partial(_attn_kernel, tq=tq, nblk=nblk),
        out_shape=jax.ShapeDtypeStruct((b, s, ATT_W), BF16),
        grid=(b, s // tq),
        in_specs=[
            _layer(sink.shape[1:], l),
            main(ATT_W),
            main(4 * KV_W), prev(4 * KV_W), nxt(4 * KV_W),
            pl.BlockSpec((vrows, tq), lambda bi, i: (0, bi * (s // tq) + i)),
            pl.BlockSpec((vrows, BLOCK),
                         lambda bi, i: (0, bi * nblk + jnp.maximum(i * nr - 1, 0))),
            pl.BlockSpec((vrows, BLOCK),
                         lambda bi, i: (0, bi * nblk + jnp.minimum((i + 1) * nr, nblk - 1))),
            _resident(bias.shape),
        ],
        out_specs=main(ATT_W),
        scratch_shapes=[pltpu.VMEM((ATTN_RING, 3 * BLOCK, 2 * LANES), F32)],
        compiler_params=pltpu.CompilerParams(
            dimension_semantics=("arbitrary", "arbitrary"), vmem_limit_bytes=VMEM_LIMIT),
        name="attn",
    )(sink, q, kx, kx, kx, vt, vt, vt, bias)


def _fourier_kernel(f_ref, l1_ref, m2_ref, cs_ref, o_ref, yr_ref, yi_ref, xbuf, *, n1, n2):
    p2 = n2 + PITCH_PAD
    p1 = n1 + PITCH_PAD
    for s2 in range(n2):
        a = f_ref[pl.ds(s2, n1, stride=p2), :].astype(BF16)
        y = _dot(l1_ref[s2], a)
        yr_ref[s2 * p1:s2 * p1 + n1, :] = y[:n1]
        yi_ref[s2 * p1:s2 * p1 + n1, :] = y[n1:]

    def stage2(kp):
        ys = [jnp.concatenate([yr_ref[pl.ds(2 * kp + d, n2, stride=p1), :],
                               yi_ref[pl.ds(2 * kp + d, n2, stride=p1), :]], axis=0)
              for d in range(2)]
        return _dot(m2_ref[...], jnp.concatenate(ys, axis=1).astype(BF16))

    nbuf = xbuf.shape[0]
    npairs = n1 // 2
    for kp in range(nbuf - 1):
        xbuf[kp] = stage2(kp)
    for kp in range(npairs):
        ahead = kp + nbuf - 1
        if ahead < npairs:
            xbuf[ahead % nbuf] = stage2(ahead)
        x = xbuf[kp % nbuf]
        xc = jnp.concatenate([
            jnp.concatenate([x[:n2, :LANES], x[n2:, :LANES]], axis=1),
            jnp.concatenate([x[:n2, LANES:], x[n2:, LANES:]], axis=1)], axis=0).astype(BF16)
        o = _dot(xc, cs_ref[...])
        o_ref[pl.ds(2 * kp, n2, stride=p1), :] = o[:n2]
        o_ref[pl.ds(2 * kp + 1, n2, stride=p1), :] = o[n2:]
    for k2 in range(n2):
        o_ref[k2 * p1 + n1:(k2 + 1) * p1, :] = jnp.zeros((PITCH_PAD, FG_W), F32)


def _fourier(f, l1, m2, cs):
    b, rows_in, _ = f.shape
    n1 = FFT_N1
    n2 = rows_in // n1 - PITCH_PAD
    rows_out = n2 * (n1 + PITCH_PAD)
    return pl.pallas_call(
        functools.partial(_fourier_kernel, n1=n1, n2=n2),
        out_shape=jax.ShapeDtypeStruct((b, rows_out, FOUR_W), F32),
        grid=(b, N_FG),
        in_specs=[pl.BlockSpec((None, rows_in, FG_W), lambda bi, g: (bi, 0, g)),
                  _resident(l1.shape), _resident(m2.shape), _resident(cs.shape)],
        out_specs=pl.BlockSpec((None, rows_out, FG_W), lambda bi, g: (bi, 0, g)),
        scratch_shapes=[pltpu.VMEM((rows_out, FG_W), F32), pltpu.VMEM((rows_out, FG_W), F32),
                        pltpu.VMEM((FOURIER_RING, 2 * n2, 2 * LANES), F32)],
        compiler_params=pltpu.CompilerParams(
            dimension_semantics=("arbitrary", "arbitrary"), vmem_limit_bytes=VMEM_LIMIT),
        name="fourier",
    )(f, l1, m2, cs)


def _fft_tables(s):
    n1, n2 = FFT_N1, s // FFT_N1
    k1 = np.arange(n1, dtype=np.int64)
    s1 = np.arange(n1, dtype=np.int64)
    s2 = np.arange(n2, dtype=np.int64)
    num = (s2[:, None, None] * k1[None, :, None]
           + n2 * s1[None, None, :] * k1[None, :, None]) % s
    th = 2.0 * np.pi * num.astype(np.float64) / s
    l1 = np.concatenate([np.cos(th), -np.sin(th)], axis=1) / np.sqrt(n1)
    k2 = np.arange(n2, dtype=np.int64)
    th2 = 2.0 * np.pi * ((k2[:, None] * s2[None, :]) % n2).astype(np.float64) / n2
    c2, sn2 = np.cos(th2), np.sin(th2)
    m2 = np.block([[c2, sn2], [-sn2, c2]]) / np.sqrt(n2)
    c = np.arange(FG_W, dtype=np.int64)
    thc = 2.0 * np.pi * ((c[:, None] * c[None, :]) % FG_W).astype(np.float64) / FG_W
    cs = np.concatenate([np.cos(thc), np.sin(thc)], axis=0) / np.sqrt(FG_W)
    as_bf16 = lambda a: jnp.asarray(a.astype(np.float32)).astype(BF16)
    return as_bf16(l1), as_bf16(m2), as_bf16(cs)


def _mixout_kernel(x_ref, ao_ref, fo_ref, gain_ref, win_ref, wa_ref, wf_ref, wo_ref, o_ref,
                   *, sub):
    p1 = FFT_N1 + PITCH_PAD
    slabs = sub // FFT_N1
    mixed = []
    for r in range(x_ref.shape[0] // sub):
        rows = slice(r * sub, (r + 1) * sub)
        h = _rms_rows(x_ref[rows, :], gain_ref[...]).astype(BF16)
        ga = jax.nn.sigmoid(_dot(h, win_ref[:, QKVF_W:QKVF_W + D_MODEL]))
        gf = jax.nn.sigmoid(_dot(h, win_ref[:, QKVF_W + D_MODEL:]))
        a = _dot(ao_ref[rows, :], wa_ref[...])
        fo = jnp.concatenate([fo_ref[(r * slabs + j) * p1:(r * slabs + j) * p1 + FFT_N1, :]
                              for j in range(slabs)], axis=0)
        fb = _dot(fo.astype(BF16), wf_ref[...])
        mixed.append((ga * a + gf * fb).astype(BF16))
    for r, m in enumerate(mixed):
        rows = slice(r * sub, (r + 1) * sub)
        o_ref[rows, :] = x_ref[rows, :] + _dot(m, wo_ref[...])


def _mixout(x, ao, fo, gain, w_in, w_attn_br, w_four_br, w_out, l, *, tm=PROJ_ROWS,
            sub=PROJ_SUB_ROWS):
    t = x.shape[0]
    row = lambda w: pl.BlockSpec((tm, w), lambda i: (i, 0))
    fo_rows = tm // FFT_N1 * (FFT_N1 + PITCH_PAD)
    return pl.pallas_call(
        functools.partial(_mixout_kernel, sub=sub),
        out_shape=jax.ShapeDtypeStruct(x.shape, F32),
        grid=(t // tm,),
        in_specs=[
            row(D_MODEL), row(ATT_W), pl.BlockSpec((fo_rows, FOUR_W), lambda i: (i, 0)),
            _layer((1, D_MODEL), l),
            _layer((D_MODEL, IN_COLS), l),
            _layer((ATT_W, D_MODEL), l), _layer((FOUR_W, D_MODEL), l),
            _layer((D_MODEL, D_MODEL), l),
        ],
        out_specs=row(D_MODEL),
        compiler_params=pltpu.CompilerParams(
            dimension_semantics=("arbitrary",), vmem_limit_bytes=VMEM_LIMIT),
        name="mixout",
    )(x, ao, fo, gain, w_in, w_attn_br, w_four_br, w_out)


def _attn_bias():
    si = np.arange(3 * BLOCK)[:, None]
    qi = np.arange(BLOCK)[None, :]
    dist = np.abs(qi + BLOCK - si).astype(np.float64)
    slopes = np.power(2.0, -8.0 * (np.arange(N_HEADS) + 1) / N_HEADS)
    out = np.empty((3, 2 * N_KV, 3 * BLOCK, 2 * LANES), np.float32)
    for edge in range(3):
        ok = dist <= WINDOW
        if edge == 1:
            ok = ok & (si >= BLOCK)
        if edge == 2:
            ok = ok & (si < 2 * BLOCK)
        for g in range(N_KV):
            for par in range(2):
                for half, h in enumerate((GROUP * g + par, GROUP * g + 2 + par)):
                    out[edge, 2 * g + par, :, half * LANES:(half + 1) * LANES] = np.where(
                        ok, -slopes[h] * dist * LOG2E, NEG)
    return jnp.asarray(out.reshape(3 * 2 * N_KV, 3 * BLOCK, 2 * LANES))


def _sink_table(sink):
    depth = sink.shape[0]
    s4 = sink.reshape(depth, N_KV, 2, 2) * LOG2E
    pairs = jnp.transpose(s4, (0, 1, 3, 2))
    return jnp.repeat(pairs.reshape(depth, 2 * N_KV, 1, 2), LANES, axis=-1)


def _trunk(x, p, consts):
    b, s, _ = x.shape
    t = b * s
    l1, m2, cs = consts["fft"][s]
    x = x.reshape(t, D_MODEL)
    for l in range(DEPTH):
        x = _ffn(x, p["ln_ffn1"], p["w_ffn1_in"], p["w_ffn1_out"], l)
        q, kx, vt, f = _inproj(x, p["ln_mix"], p["w_in"], p["q_gain"], p["k_gain"], l,
                               n2=s // FFT_N1)
        ao = _attn(q.reshape(b, s, ATT_W), kx.reshape(b, s, 4 * KV_W), vt,
                   consts["bias"], p["sink"], l)
        fo = _fourier(f.reshape(b, -1, FOUR_W), l1, m2, cs)
        x = _mixout(x, ao.reshape(t, ATT_W), fo.reshape(-1, FOUR_W), p["ln_mix"], p["w_in"],
                    p["w_attn_br"], p["w_four_br"], p["w_out"], l)
        x = _ffn(x, p["ln_ffn2"], p["w_ffn2_in"], p["w_ffn2_out"], l)
    return x.reshape(b, s, D_MODEL)


def kernel(x_prompt, x_sample, ln_ffn1, w_ffn1_in, w_ffn1_out, ln_mix, w_in, q_gain, k_gain,
           sink, w_attn_br, w_four_br, w_out, ln_ffn2, w_ffn2_in, w_ffn2_out):
    depth = ln_ffn1.shape[0]
    p = {
        "ln_ffn1": ln_ffn1.reshape(depth, 1, D_MODEL),
        "ln_mix": ln_mix.reshape(depth, 1, D_MODEL),
        "ln_ffn2": ln_ffn2.reshape(depth, 1, D_MODEL),
        "w_ffn1_in": w_ffn1_in.astype(BF16), "w_ffn1_out": w_ffn1_out.astype(BF16),
        "w_ffn2_in": w_ffn2_in.astype(BF16), "w_ffn2_out": w_ffn2_out.astype(BF16),
        "w_in": w_in.astype(BF16),
        "q_gain": jnp.tile(q_gain, (1, N_HEADS)).reshape(depth, 1, ATT_W),
        "k_gain": jnp.tile(k_gain, (1, N_KV)).reshape(depth, 1, KV_W),
        "sink": _sink_table(sink),
        "w_attn_br": w_attn_br.astype(BF16), "w_four_br": w_four_br.astype(BF16),
        "w_out": w_out.astype(BF16),
    }
    consts = {
        "bias": _attn_bias(),
        "fft": {s: _fft_tables(s) for s in {x_prompt.shape[1], x_sample.shape[1]}},
    }
    return (_trunk(x_prompt, p, consts), _trunk(x_sample, p, consts))
```

```python
import functools

import numpy as np
import jax
import jax.numpy as jnp
from jax import lax
from jax.experimental import pallas as pl
from jax.experimental.pallas import tpu as pltpu

D_MODEL = 1024
DEPTH = 4
HEAD_DIM = 64
N_HEADS = 8
N_KV = 2
GROUP = N_HEADS // N_KV
ATT_W = N_HEADS * HEAD_DIM
KV_W = N_KV * HEAD_DIM
N_FG = 4
FG_W = 128
FOUR_W = N_FG * FG_W
QKVF_W = ATT_W + 2 * KV_W + FOUR_W
IN_COLS = QKVF_W + 2 * D_MODEL
D_FF = 2816
WINDOW = 128
BLOCK = 128
EPS = 1e-6
NEG = -1e30

LANES = 128
LOG2E = 1.4426950408889634
Q_SCALE = HEAD_DIM ** -0.5 * LOG2E
VT_ONES = 16
VT_ROWS = HEAD_DIM + VT_ONES
FFT_N1 = 64
PITCH_PAD = 8
VMEM_LIMIT = 56 * 1024 * 1024

FFN_ROWS = 1024
FFN_CHUNK = 256
PROJ_ROWS = 1024
PROJ_SUB_ROWS = 512
ATTN_ROWS = 4096
ATTN_RING = 6
FOURIER_RING = 4

BF16 = jnp.bfloat16
F32 = jnp.float32


def _dot(a, b):
    return jnp.dot(a, b, preferred_element_type=F32)


def _rms_rows(x, gain):
    ms = jnp.mean(x * x, axis=-1, keepdims=True)
    return x * lax.rsqrt(ms + EPS) * gain


def _resident(shape):
    nd = len(shape)
    return pl.BlockSpec(shape, lambda *_: (0,) * nd, pipeline_mode=pl.Buffered(1))


def _layer(shape, l, col=0):
    mid = (0,) * (len(shape) - 1)
    return pl.BlockSpec((None,) + tuple(shape), lambda *_: (l,) + mid + (col,),
                        pipeline_mode=pl.Buffered(1))


def _ffn_kernel(x_ref, gain_ref, wg_ref, wu_ref, wo_ref, o_ref, a_ref, *, chunk):
    x = x_ref[...]
    h = _rms_rows(x, gain_ref[...]).astype(BF16)
    for c in range(D_FF // chunk):
        sl = slice(c * chunk, (c + 1) * chunk)
        g = _dot(h, wg_ref[:, sl])
        u = _dot(h, wu_ref[:, sl])
        a_ref[:, sl] = (jax.nn.silu(g) * u).astype(BF16)
    o_ref[...] = x + 0.5 * _dot(a_ref[...], wo_ref[...])


def _ffn(x, gain, w_in, w_out, l, *, tm=FFN_ROWS, chunk=FFN_CHUNK):
    t = x.shape[0]
    return pl.pallas_call(
        functools.partial(_ffn_kernel, chunk=chunk),
        out_shape=jax.ShapeDtypeStruct(x.shape, F32),
        grid=(t // tm,),
        in_specs=[
            pl.BlockSpec((tm, D_MODEL), lambda i: (i, 0)),
            _layer((1, D_MODEL), l),
            _layer((D_MODEL, D_FF), l, 0),
            _layer((D_MODEL, D_FF), l, 1),
            _layer((D_FF, D_MODEL), l),
        ],
        out_specs=pl.BlockSpec((tm, D_MODEL), lambda i: (i, 0)),
        scratch_shapes=[pltpu.VMEM((tm, D_FF), BF16)],
        compiler_params=pltpu.CompilerParams(
            dimension_semantics=("arbitrary",), vmem_limit_bytes=VMEM_LIMIT),
        name="ffn",
    )(x, gain, w_in, w_in, w_out)


def _head_rms(t, gain):
    sq = t * t
    low = lax.broadcasted_iota(jnp.int32, (t.shape[0], LANES), 1) < HEAD_DIM
    tiles = []
    for c in range(0, t.shape[1], LANES):
        pair = sq[:, c:c + LANES]
        lo = jnp.sum(jnp.where(low, pair, 0.0), axis=-1, keepdims=True)
        hi = jnp.sum(jnp.where(low, 0.0, pair), axis=-1, keepdims=True)
        tiles.append(jnp.where(low, lo, hi))
    ms = jnp.concatenate(tiles, axis=1) * (1.0 / HEAD_DIM)
    return t * lax.rsqrt(ms + EPS) * gain


def _inproj_kernel(x_ref, gain_ref, w_ref, qg_ref, kg_ref, q_ref, kx_ref, vt_ref, f_ref,
                   *, n2, sub):
    nsub = x_ref.shape[0] // sub
    zs = []
    for r in range(nsub):
        x = x_ref[r * sub:(r + 1) * sub, :]
        zs.append(_dot(_rms_rows(x, gain_ref[...]).astype(BF16), w_ref[...]))
    p2 = n2 + PITCH_PAD
    for r, z in enumerate(zs):
        rows = slice(r * sub, (r + 1) * sub)
        q = z[:, :ATT_W]
        k = z[:, ATT_W:ATT_W + KV_W]
        v = z[:, ATT_W + KV_W:ATT_W + 2 * KV_W]
        f = z[:, ATT_W + 2 * KV_W:]
        for j in range(sub // n2):
            base = (r * (sub // n2) + j) * p2
            f_ref[base:base + n2, :] = f[j * n2:(j + 1) * n2]
            f_ref[base + n2:base + p2, :] = jnp.zeros((PITCH_PAD, FOUR_W), F32)
        q_ref[rows, :] = (_head_rms(q, qg_ref[...]) * Q_SCALE).astype(BF16)

        kn = _head_rms(k, kg_ref[...])
        lane = lax.broadcasted_iota(jnp.int32, kn.shape, 1)
        low = lane < HEAD_DIM
        ksw = pltpu.roll(kn, HEAD_DIM, axis=1)
        zero = jnp.zeros_like(kn)
        kx = jnp.concatenate([
            jnp.where(low, kn, zero),
            jnp.where(low, zero, ksw),
            jnp.where(low, ksw, zero),
            jnp.where(low, zero, kn),
        ], axis=1)
        kx_ref[rows, :] = kx.astype(BF16)

        vt = v.T
        ones = jnp.ones((VT_ONES, sub), F32)
        vt_ref[:, rows] = jnp.concatenate(
            [vt[:HEAD_DIM], ones, vt[HEAD_DIM:], ones], axis=0).astype(BF16)


def _inproj(x, gain, w_in, q_gain, k_gain, l, *, n2, tm=PROJ_ROWS, sub=PROJ_SUB_ROWS):
    t = x.shape[0]
    assert tm % sub == 0 and sub % n2 == 0
    row = lambda w: pl.BlockSpec((tm, w), lambda i: (i, 0))
    f_rows = tm // n2 * (n2 + PITCH_PAD)
    return pl.pallas_call(
        functools.partial(_inproj_kernel, n2=n2, sub=sub),
        out_shape=(
            jax.ShapeDtypeStruct((t, ATT_W), BF16),
            jax.ShapeDtypeStruct((t, 4 * KV_W), BF16),
            jax.ShapeDtypeStruct((N_KV * VT_ROWS, t), BF16),
            jax.ShapeDtypeStruct((t // tm * f_rows, FOUR_W), F32),
        ),
        grid=(t // tm,),
        in_specs=[
            row(D_MODEL),
            _layer((1, D_MODEL), l),
            _layer((D_MODEL, QKVF_W), l),
            _layer((1, ATT_W), l),
            _layer((1, KV_W), l),
        ],
        out_specs=(row(ATT_W), row(4 * KV_W),
                   pl.BlockSpec((N_KV * VT_ROWS, tm), lambda i: (0, i)),
                   pl.BlockSpec((f_rows, FOUR_W), lambda i: (i, 0))),
        compiler_params=pltpu.CompilerParams(
            dimension_semantics=("arbitrary",), vmem_limit_bytes=VMEM_LIMIT),
        name="inproj",
    )(x, gain, w_in, q_gain, k_gain)


def _attn_kernel(sink_ref, q_ref, kx_ref, kxp_ref, kxn_ref, vt_ref, vtp_ref, vtn_ref,
                 bias_ref, o_ref, sbuf, *, tq, nblk):
    i = pl.program_id(1)
    nr = tq // BLOCK

    def window(r, main, prev, nxt, axis, other):
        def cut(ref, lo, hi):
            return ref[lo:hi, other] if axis == 0 else ref[other, lo:hi]
        if r == 0:
            parts = [cut(prev, 0, BLOCK), cut(main, 0, 2 * BLOCK)]
        elif r == nr - 1:
            parts = [cut(main, (r - 1) * BLOCK, (r + 1) * BLOCK), cut(nxt, 0, BLOCK)]
        else:
            return cut(main, (r - 1) * BLOCK, (r + 2) * BLOCK)
        return jnp.concatenate(parts, axis=axis)

    def scores(r, pair):
        g = pair // 2
        rows = slice(r * BLOCK, (r + 1) * BLOCK)
        qpair = jnp.concatenate(
            [q_ref[rows, (2 * g) * LANES:(2 * g + 1) * LANES],
             q_ref[rows, (2 * g + 1) * LANES:(2 * g + 2) * LANES]], axis=0)
        kw = window(r, kx_ref, kxp_ref, kxn_ref, 0, slice(pair * LANES, (pair + 1) * LANES))
        return lax.dot_general(kw, qpair, (((1,), (1,)), ((), ())),
                               preferred_element_type=F32)

    items = [(r, pair) for r in range(nr) for pair in range(2 * N_KV)]
    nbuf = sbuf.shape[0]
    for n in range(nbuf - 1):
        sbuf[n] = scores(*items[n])
    parts = [None] * N_HEADS
    for n, (r, pair) in enumerate(items):
        ahead = n + nbuf - 1
        if ahead < len(items):
            sbuf[ahead % nbuf] = scores(*items[ahead])
        s = sbuf[n % nbuf]
        g, par = pair // 2, pair % 2
        if r == 0:
            edge = jnp.where(i == 0, 1, 0)
        elif r == nr - 1:
            edge = jnp.where(i * nr + r == nblk - 1, 2, 0)
        else:
            edge = 0
        s = s + bias_ref[edge * (2 * N_KV) + pair]
        sink = sink_ref[pair]
        m = jnp.maximum(jnp.max(s, axis=0, keepdims=True), sink)
        p = jnp.exp2(s - m).astype(BF16)
        vt = window(r, vt_ref, vtp_ref, vtn_ref, 1, slice(g * VT_ROWS, (g + 1) * VT_ROWS))
        oe = _dot(vt, p)
        den = oe[HEAD_DIM:HEAD_DIM + 1] + jnp.exp2(sink - m)
        o = oe[:HEAD_DIM] * (1.0 / den)
        parts[GROUP * g + par] = o[:, :LANES]
        parts[GROUP * g + 2 + par] = o[:, LANES:]
        if pair == 2 * N_KV - 1:
            o_ref[r * BLOCK:(r + 1) * BLOCK, :] = jnp.concatenate(parts, axis=0).T.astype(BF16)


def _attn(q, kx, vt, bias, sink, l, *, tq=ATTN_ROWS):
    b, s, _ = q.shape
    nr = tq // BLOCK
    nblk = s // BLOCK
    assert nr >= 2 and s % tq == 0
    main = lambda w: pl.BlockSpec((None, tq, w), lambda bi, i: (bi, i, 0))
    prev = lambda w: pl.BlockSpec(
        (None, BLOCK, w), lambda bi, i: (bi, jnp.maximum(i * nr - 1, 0), 0))
    nxt = lambda w: pl.BlockSpec(
        (None, BLOCK, w), lambda bi, i: (bi, jnp.minimum((i + 1) * nr, nblk - 1), 0))
    vrows = N_KV * VT_ROWS
    return pl.pallas_call(
        functools.partial(_attn_kernel, tq=tq, nblk=nblk),
        out_shape=jax.ShapeDtypeStruct((b, s, ATT_W), BF16),
        grid=(b, s // tq),
        in_specs=[
            _layer(sink.shape[1:], l),
            main(ATT_W),
            main(4 * KV_W), prev(4 * KV_W), nxt(4 * KV_W),
            pl.BlockSpec((vrows, tq), lambda bi, i: (0, bi * (s // tq) + i)),
            pl.BlockSpec((vrows, BLOCK),
                         lambda bi, i: (0, bi * nblk + jnp.maximum(i * nr - 1, 0))),
            pl.BlockSpec((vrows, BLOCK),
                         lambda bi, i: (0, bi * nblk + jnp.minimum((i + 1) * nr, nblk - 1))),
            _resident(bias.shape),
        ],
        out_specs=main(ATT_W),
        scratch_shapes=[pltpu.VMEM((ATTN_RING, 3 * BLOCK, 2 * LANES), F32)],
        compiler_params=pltpu.CompilerParams(
            dimension_semantics=("arbitrary", "arbitrary"), vmem_limit_bytes=VMEM_LIMIT),
        name="attn",
    )(sink, q, kx, kx, kx, vt, vt, vt, bias)


def _fourier_kernel(f_ref, l1_ref, m2_ref, cs_ref, o_ref, yr_ref, yi_ref, xbuf, *, n1, n2):
    p2 = n2 + PITCH_PAD
    p1 = n1 + PITCH_PAD
    for s2 in range(n2):
        a = f_ref[pl.ds(s2, n1, stride=p2), :].astype(BF16)
        y = _dot(l1_ref[s2], a)
        yr_ref[s2 * p1:s2 * p1 + n1, :] = y[:n1]
        yi_ref[s2 * p1:s2 * p1 + n1, :] = y[n1:]

    def stage2(kp):
        ys = [jnp.concatenate([yr_ref[pl.ds(2 * kp + d, n2, stride=p1), :],
                               yi_ref[pl.ds(2 * kp + d, n2, stride=p1), :]], axis=0)
              for d in range(2)]
        return _dot(m2_ref[...], jnp.concatenate(ys, axis=1).astype(BF16))

    nbuf = xbuf.shape[0]
    npairs = n1 // 2
    for kp in range(nbuf - 1):
        xbuf[kp] = stage2(kp)
    for kp in range(npairs):
        ahead = kp + nbuf - 1
        if ahead < npairs:
            xbuf[ahead % nbuf] = stage2(ahead)
        x = xbuf[kp % nbuf]
        xc = jnp.concatenate([
            jnp.concatenate([x[:n2, :LANES], x[n2:, :LANES]], axis=1),
            jnp.concatenate([x[:n2, LANES:], x[n2:, LANES:]], axis=1)], axis=0).astype(BF16)
        o = _dot(xc, cs_ref[...])
        o_ref[pl.ds(2 * kp, n2, stride=p1), :] = o[:n2]
        o_ref[pl.ds(2 * kp + 1, n2, stride=p1), :] = o[n2:]
    for k2 in range(n2):
        o_ref[k2 * p1 + n1:(k2 + 1) * p1, :] = jnp.zeros((PITCH_PAD, FG_W), F32)


def _fourier(f, l1, m2, cs):
    b, rows_in, _ = f.shape
    n1 = FFT_N1
    n2 = rows_in // n1 - PITCH_PAD
    rows_out = n2 * (n1 + PITCH_PAD)
    return pl.pallas_call(
        functools.partial(_fourier_kernel, n1=n1, n2=n2),
        out_shape=jax.ShapeDtypeStruct((b, rows_out, FOUR_W), F32),
        grid=(b, N_FG),
        in_specs=[pl.BlockSpec((None, rows_in, FG_W), lambda bi, g: (bi, 0, g)),
                  _resident(l1.shape), _resident(m2.shape), _resident(cs.shape)],
        out_specs=pl.BlockSpec((None, rows_out, FG_W), lambda bi, g: (bi, 0, g)),
        scratch_shapes=[pltpu.VMEM((rows_out, FG_W), F32), pltpu.VMEM((rows_out, FG_W), F32),
                        pltpu.VMEM((FOURIER_RING, 2 * n2, 2 * LANES), F32)],
        compiler_params=pltpu.CompilerParams(
            dimension_semantics=("arbitrary", "arbitrary"), vmem_limit_bytes=VMEM_LIMIT),
        name="fourier",
    )(f, l1, m2, cs)


def _fft_tables(s):
    n1, n2 = FFT_N1, s // FFT_N1
    k1 = np.arange(n1, dtype=np.int64)
    s1 = np.arange(n1, dtype=np.int64)
    s2 = np.arange(n2, dtype=np.int64)
    num = (s2[:, None, None] * k1[None, :, None]
           + n2 * s1[None, None, :] * k1[None, :, None]) % s
    th = 2.0 * np.pi * num.astype(np.float64) / s
    l1 = np.concatenate([np.cos(th), -np.sin(th)], axis=1) / np.sqrt(n1)
    k2 = np.arange(n2, dtype=np.int64)
    th2 = 2.0 * np.pi * ((k2[:, None] * s2[None, :]) % n2).astype(np.float64) / n2
    c2, sn2 = np.cos(th2), np.sin(th2)
    m2 = np.block([[c2, sn2], [-sn2, c2]]) / np.sqrt(n2)
    c = np.arange(FG_W, dtype=np.int64)
    thc = 2.0 * np.pi * ((c[:, None] * c[None, :]) % FG_W).astype(np.float64) / FG_W
    cs = np.concatenate([np.cos(thc), np.sin(thc)], axis=0) / np.sqrt(FG_W)
    as_bf16 = lambda a: jnp.asarray(a.astype(np.float32)).astype(BF16)
    return as_bf16(l1), as_bf16(m2), as_bf16(cs)


def _mixout_kernel(x_ref, ao_ref, fo_ref, gain_ref, win_ref, wa_ref, wf_ref, wo_ref, o_ref,
                   *, sub):
    p1 = FFT_N1 + PITCH_PAD
    slabs = sub // FFT_N1
    mixed = []
    for r in range(x_ref.shape[0] // sub):
        rows = slice(r * sub, (r + 1) * sub)
        h = _rms_rows(x_ref[rows, :], gain_ref[...]).astype(BF16)
        ga = jax.nn.sigmoid(_dot(h, win_ref[:, QKVF_W:QKVF_W + D_MODEL]))
        gf = jax.nn.sigmoid(_dot(h, win_ref[:, QKVF_W + D_MODEL:]))
        a = _dot(ao_ref[rows, :], wa_ref[...])
        fo = jnp.concatenate([fo_ref[(r * slabs + j) * p1:(r * slabs + j) * p1 + FFT_N1, :]
                              for j in range(slabs)], axis=0)
        fb = _dot(fo.astype(BF16), wf_ref[...])
        mixed.append((ga * a + gf * fb).astype(BF16))
    for r, m in enumerate(mixed):
        rows = slice(r * sub, (r + 1) * sub)
        o_ref[rows, :] = x_ref[rows, :] + _dot(m, wo_ref[...])


def _mixout(x, ao, fo, gain, w_in, w_attn_br, w_four_br, w_out, l, *, tm=PROJ_ROWS,
            sub=PROJ_SUB_ROWS):
    t = x.shape[0]
    row = lambda w: pl.BlockSpec((tm, w), lambda i: (i, 0))
    fo_rows = tm // FFT_N1 * (FFT_N1 + PITCH_PAD)
    return pl.pallas_call(
        functools.partial(_mixout_kernel, sub=sub),
        out_shape=jax.ShapeDtypeStruct(x.shape, F32),
        grid=(t // tm,),
        in_specs=[
            row(D_MODEL), row(ATT_W), pl.BlockSpec((fo_rows, FOUR_W), lambda i: (i, 0)),
            _layer((1, D_MODEL), l),
            _layer((D_MODEL, IN_COLS), l),
            _layer((ATT_W, D_MODEL), l), _layer((FOUR_W, D_MODEL), l),
            _layer((D_MODEL, D_MODEL), l),
        ],
        out_specs=row(D_MODEL),
        compiler_params=pltpu.CompilerParams(
            dimension_semantics=("arbitrary",), vmem_limit_bytes=VMEM_LIMIT),
        name="mixout",
    )(x, ao, fo, gain, w_in, w_attn_br, w_four_br, w_out)


def _attn_bias():
    si = np.arange(3 * BLOCK)[:, None]
    qi = np.arange(BLOCK)[None, :]
    dist = np.abs(qi + BLOCK - si).astype(np.float64)
    slopes = np.power(2.0, -8.0 * (np.arange(N_HEADS) + 1) / N_HEADS)
    out = np.empty((3, 2 * N_KV, 3 * BLOCK, 2 * LANES), np.float32)
    for edge in range(3):
        ok = dist <= WINDOW
        if edge == 1:
            ok = ok & (si >= BLOCK)
        if edge == 2:
            ok = ok & (si < 2 * BLOCK)
        for g in range(N_KV):
            for par in range(2):
                for half, h in enumerate((GROUP * g + par, GROUP * g + 2 + par)):
                    out[edge, 2 * g + par, :, half * LANES:(half + 1) * LANES] = np.where(
                        ok, -slopes[h] * dist * LOG2E, NEG)
    return jnp.asarray(out.reshape(3 * 2 * N_KV, 3 * BLOCK, 2 * LANES))


def _sink_table(sink):
    depth = sink.shape[0]
    s4 = sink.reshape(depth, N_KV, 2, 2) * LOG2E
    pairs = jnp.transpose(s4, (0, 1, 3, 2))
    return jnp.repeat(pairs.reshape(depth, 2 * N_KV, 1, 2), LANES, axis=-1)


def _trunk(x, p, consts):
    b, s, _ = x.shape
    t = b * s
    l1, m2, cs = consts["fft"][s]
    x = x.reshape(t, D_MODEL)
    for l in range(DEPTH):
        x = _ffn(x, p["ln_ffn1"], p["w_ffn1_in"], p["w_ffn1_out"], l)
        q, kx, vt, f = _inproj(x, p["ln_mix"], p["w_in"], p["q_gain"], p["k_gain"], l,
                               n2=s // FFT_N1)
        ao = _attn(q.reshape(b, s, ATT_W), kx.reshape(b, s, 4 * KV_W), vt,
                   consts["bias"], p["sink"], l)
        fo = _fourier(f.reshape(b, -1, FOUR_W), l1, m2, cs)
        x = _mixout(x, ao.reshape(t, ATT_W), fo.reshape(-1, FOUR_W), p["ln_mix"], p["w_in"],
                    p["w_attn_br"], p["w_four_br"], p["w_out"], l)
        x = _ffn(x, p["ln_ffn2"], p["w_ffn2_in"], p["w_ffn2_out"], l)
    return x.reshape(b, s, D_MODEL)


def kernel(x_prompt, x_sample, ln_ffn1, w_ffn1_in, w_ffn1_out, ln_mix, w_in, q_gain, k_gain,
           sink, w_attn_br, w_four_br, w_out, ln_ffn2, w_ffn2_in, w_ffn2_out):
    depth = ln_ffn1.shape[0]
    p = {
        "ln_ffn1": ln_ffn1.reshape(depth, 1, D_MODEL),
        "ln_mix": ln_mix.reshape(depth, 1, D_MODEL),
        "ln_ffn2": ln_ffn2.reshape(depth, 1, D_MODEL),
        "w_ffn1_in": w_ffn1_in.astype(BF16), "w_ffn1_out": w_ffn1_out.astype(BF16),
        "w_ffn2_in": w_ffn2_in.astype(BF16), "w_ffn2_out": w_ffn2_out.astype(BF16),
        "w_in": w_in.astype(BF16),
        "q_gain": jnp.tile(q_gain, (1, N_HEADS)).reshape(depth, 1, ATT_W),
        "k_gain": jnp.tile(k_gain, (1, N_KV)).reshape(depth, 1, KV_W),
        "sink": _sink_table(sink),
        "w_attn_br": w_attn_br.astype(BF16), "w_four_br": w_four_br.astype(BF16),
        "w_out": w_out.astype(BF16),
    }
    consts = {
        "bias": _attn_bias(),
        "fft": {s: _fft_tables(s) for s in {x_prompt.shape[1], x_sample.shape[1]}},
    }
    return (_trunk(x_prompt, p, consts), _trunk(x_sample, p, consts))
```
